```python
import functools
import jax, jax.numpy as jnp
from jax import lax
import numpy as np

D_MODEL = 1024
BATCH = 4
SEQ = 8192
DEPTH = 2
DEC_BATCH = 32
DEC_SEQ = 1
PAST_LEN = 16384
PAGE_SIZE = 128

N_HEADS = 8
HEAD_DIM = 64
D_ATTN = N_HEADS * HEAD_DIM
D_CONV = D_MODEL // 4
CONV_WIDTH = 31
D_SCONV = D_MODEL // 4
SCONV_WIDTH = 3
D_MIX = D_ATTN + D_CONV + D_SCONV
WINDOWS = (128, 512, 2048)
DILATIONS = (1, 4, 16)
MAX_WINDOW = 2048
ROPE_THETA = 10000.0
EPS = 1e-6
Q_BLOCK = 128
IN_SIZES = (D_ATTN, D_ATTN, D_ATTN, D_ATTN,
            D_CONV, D_CONV, D_CONV,
            D_SCONV, D_SCONV, D_SCONV, D_SCONV)
D_IN = 4 * D_ATTN + 3 * D_CONV + 4 * D_SCONV

kernel_name = 'hymba_longnet_conformer_shortconv_step'


def rmsnorm(x, g):
    xf = x.astype(jnp.float32)
    y = xf * lax.rsqrt(jnp.mean(xf * xf, axis=-1, keepdims=True) + EPS)
    return (y * g.astype(jnp.float32)).astype(x.dtype)


def layernorm(x, g, b):
    xf = x.astype(jnp.float32)
    mu = jnp.mean(xf, axis=-1, keepdims=True)
    var = jnp.mean(jnp.square(xf - mu), axis=-1, keepdims=True)
    y = (xf - mu) * lax.rsqrt(var + EPS) * g.astype(jnp.float32) + b.astype(jnp.float32)
    return y.astype(x.dtype)


def rope(x, pos):
    half = HEAD_DIM // 2
    inv = ROPE_THETA ** (-jnp.arange(half, dtype=jnp.float32) / half)
    ang = pos.astype(jnp.float32)[:, None] * inv[None, :]
    cos = jnp.cos(ang)[None, :, None, :]
    sin = jnp.sin(ang)[None, :, None, :]
    xf = x.astype(jnp.float32)
    x1, x2 = xf[..., :half], xf[..., half:]
    return jnp.concatenate([x1 * cos - x2 * sin, x2 * cos + x1 * sin], axis=-1).astype(x.dtype)


def causal_dwconv(u_full, w):
    c = u_full.shape[-1]
    return lax.conv_general_dilated(u_full, w[:, None, :].astype(u_full.dtype), (1,), 'VALID',
                                    dimension_numbers=('NWC', 'WIO', 'NWC'),
                                    feature_group_count=c)


def longnet_attend(q, k_all, v_all, self_idx):
    qf = q.astype(jnp.float32) * (HEAD_DIM ** -0.5)
    scores, vals = [], []
    for w, d in zip(WINDOWS, DILATIONS):
        offs = d * jnp.arange(w // d + 1, dtype=jnp.int32)
        idx = self_idx[:, None] - offs[None, :]
        valid = idx >= 0
        idxc = jnp.maximum(idx, 0)
        kg = k_all[:, idxc]
        vals.append(v_all[:, idxc])
        s = jnp.einsum('bqhd,bqjhd->bhqj', qf, kg.astype(jnp.float32))
        scores.append(jnp.where(valid[None, None], s, -jnp.inf))
    m = functools.reduce(jnp.maximum, [s.max(axis=-1, keepdims=True) for s in scores])
    outs, dens = [], []
    for s, vg in zip(scores, vals):
        e = jnp.exp(s - m)
        den = e.sum(axis=-1).transpose(0, 2, 1)[..., None]
        num = jnp.einsum('bhqj,bqjhd->bqhd', e, vg.astype(jnp.float32))
        outs.append(num / den)
        dens.append(den)
    den_tot = functools.reduce(jnp.add, dens)
    out = functools.reduce(jnp.add, [(dn / den_tot) * o for dn, o in zip(dens, outs)])
    return out.astype(q.dtype)


def prompt_attention(q, k, v):
    b, s = q.shape[:2]
    nb = s // Q_BLOCK
    qb = q.reshape(b, nb, Q_BLOCK, N_HEADS, HEAD_DIM).transpose(1, 0, 2, 3, 4)
    starts = jnp.arange(nb, dtype=jnp.int32) * Q_BLOCK

    def one_block(args):
        qi, t0 = args
        return longnet_attend(qi, k, v, t0 + jnp.arange(Q_BLOCK, dtype=jnp.int32))

    o = lax.map(one_block, (qb, starts))
    return o.transpose(1, 0, 2, 3, 4).reshape(b, s, N_HEADS, HEAD_DIM)


def in_proj(x, pos, ln_g, w_in, qn_g, kn_g):
    b, t = x.shape[:2]
    xn = rmsnorm(x, ln_g)
    p = xn @ w_in
    cuts = [int(c) for c in np.cumsum(IN_SIZES)[:-1]]
    q, k, v, z_attn, a_val, a_gate, z_a, c_b, c_c, c_h, z_c = jnp.split(p, cuts, axis=-1)
    q = rope(rmsnorm(q.reshape(b, t, N_HEADS, HEAD_DIM), qn_g), pos)
    k = rope(rmsnorm(k.reshape(b, t, N_HEADS, HEAD_DIM), kn_g), pos)
    v = v.reshape(b, t, N_HEADS, HEAD_DIM)
    return q, k, v, z_attn, (a_val, a_gate, z_a), (c_b, c_c, c_h, z_c)


def conv_branches(a_parts, c_parts, hist_a, hist_c, a_w, a_b, a_g, a_lb, c_w):
    a_val, a_gate, z_a = a_parts
    u = a_val * jax.nn.sigmoid(a_gate)
    ua = jnp.concatenate([hist_a.astype(u.dtype), u], axis=1)
    ca = causal_dwconv(ua, a_w) + a_b.astype(u.dtype)
    ya = jax.nn.silu(layernorm(ca, a_g, a_lb)) * jax.nn.silu(z_a)
    c_b, c_c, c_h, z_c = c_parts
    uc = c_c * c_h
    ucf = jnp.concatenate([hist_c.astype(uc.dtype), uc], axis=1)
    yc = c_b * causal_dwconv(ucf, c_w) * jax.nn.silu(z_c)
    return ya, yc, ua[:, -(CONV_WIDTH - 1):], ucf[:, -(SCONV_WIDTH - 1):]


def out_proj(x, attn, z_attn, ya, yc, w_out):
    b, t = x.shape[:2]
    ya_attn = attn.reshape(b, t, D_ATTN) * jax.nn.silu(z_attn)
    return x + jnp.concatenate([ya_attn, ya, yc], axis=-1) @ w_out


def setup_inputs(seed: int = 0) -> dict:
    key = jax.random.key(seed)
    ks = jax.random.split(key, 20)
    f32 = jnp.float32
    l_cache = min(MAX_WINDOW, PAST_LEN)
    nrm = lambda k, shape, s: jax.random.normal(k, shape, f32) * s
    return {
        'x_prompt': nrm(ks[0], (BATCH, SEQ, D_MODEL), 1.0),
        'x_sample': nrm(ks[1], (DEC_BATCH, DEC_SEQ, D_MODEL), 1.0),
        'cache_k': nrm(ks[2], (DEPTH, DEC_BATCH, l_cache, N_HEADS, HEAD_DIM), 1.0),
        'cache_v': nrm(ks[3], (DEPTH, DEC_BATCH, l_cache, N_HEADS, HEAD_DIM), 1.0),
        'state_conv_a': nrm(ks[4], (DEPTH, DEC_BATCH, CONV_WIDTH - 1, D_CONV), 0.5),
        'state_conv_c': nrm(ks[5], (DEPTH, DEC_BATCH, SCONV_WIDTH - 1, D_SCONV), 0.5),
        'ln_g': 1.0 + nrm(ks[6], (DEPTH, D_MODEL), 0.01),
        'w_in': nrm(ks[7], (DEPTH, D_MODEL, D_IN), D_MODEL ** -0.5),
        'q_norm_g': 1.0 + nrm(ks[8], (DEPTH, HEAD_DIM), 0.01),
        'k_norm_g': 1.0 + nrm(ks[9], (DEPTH, HEAD_DIM), 0.01),
        'a_conv_w': nrm(ks[10], (DEPTH, CONV_WIDTH, D_CONV), CONV_WIDTH ** -0.5),
        'a_conv_b': nrm(ks[11], (DEPTH, D_CONV), 0.02),
        'a_ln_g': 1.0 + nrm(ks[12], (DEPTH, D_CONV), 0.01),
        'a_ln_b': nrm(ks[13], (DEPTH, D_CONV), 0.02),
        'c_conv_w': nrm(ks[14], (DEPTH, SCONV_WIDTH, D_SCONV), SCONV_WIDTH ** -0.5),
        'w_out': nrm(ks[15], (DEPTH, D_MIX, D_MODEL), D_MIX ** -0.5),
    }


def reference(x_prompt, x_sample, cache_k, cache_v, state_conv_a, state_conv_c,
              ln_g, w_in, q_norm_g, k_norm_g, a_conv_w, a_conv_b, a_ln_g, a_ln_b,
              c_conv_w, w_out):
    bp, sp = x_prompt.shape[:2]
    ts = x_sample.shape[1]
    l_prompt = min(MAX_WINDOW, sp)
    l_cache = cache_k.shape[2]
    pos_p = jnp.arange(sp, dtype=jnp.int32)
    pos_s = PAST_LEN + jnp.arange(ts, dtype=jnp.int32)
    hp, hs = x_prompt, x_sample
    pk, pv, pa, pc, sk, sv, sa, sc = [], [], [], [], [], [], [], []
    for l in range(DEPTH):
        q, k, v, z, ap, cp = in_proj(hp, pos_p, ln_g[l], w_in[l], q_norm_g[l], k_norm_g[l])
        attn = prompt_attention(q, k, v)
        zero_a = jnp.zeros((bp, CONV_WIDTH - 1, D_CONV), hp.dtype)
        zero_c = jnp.zeros((bp, SCONV_WIDTH - 1, D_SCONV), hp.dtype)
        ya, yc, na, nc = conv_branches(ap, cp, zero_a, zero_c, a_conv_w[l], a_conv_b[l],
                                       a_ln_g[l], a_ln_b[l], c_conv_w[l])
        hp = out_proj(hp, attn, z, ya, yc, w_out[l])
        pk.append(k[:, sp - l_prompt:])
        pv.append(v[:, sp - l_prompt:])
        pa.append(na)
        pc.append(nc)
        q, k, v, z, ap, cp = in_proj(hs, pos_s, ln_g[l], w_in[l], q_norm_g[l], k_norm_g[l])
        k_all = jnp.concatenate([cache_k[l].astype(k.dtype), k], axis=1)
        v_all = jnp.concatenate([cache_v[l].astype(v.dtype), v], axis=1)
        attn = longnet_attend(q, k_all, v_all, l_cache + jnp.arange(ts, dtype=jnp.int32))
        ya, yc, na, nc = conv_branches(ap, cp, state_conv_a[l], state_conv_c[l], a_conv_w[l],
                                       a_conv_b[l], a_ln_g[l], a_ln_b[l], c_conv_w[l])
        hs = out_proj(hs, attn, z, ya, yc, w_out[l])
        sk.append(k_all[:, -l_cache:])
        sv.append(v_all[:, -l_cache:])
        sa.append(na)
        sc.append(nc)
    return (hp, hs, jnp.stack(pk), jnp.stack(pv), jnp.stack(pa), jnp.stack(pc),
            jnp.stack(sk), jnp.stack(sv), jnp.stack(sa), jnp.stack(sc))
```

```python
import functools

import numpy as np
import jax
import jax.numpy as jnp
from jax import lax
from jax.experimental import pallas as pl
from jax.experimental.pallas import tpu as pltpu

D_MODEL = 1024
N_HEADS = 8
HEAD_DIM = 64
D_ATTN = N_HEADS * HEAD_DIM
D_CONV = 256
CONV_WIDTH = 31
D_SCONV = 256
SCONV_WIDTH = 3
WINDOWS = (128, 512, 2048)
DILATIONS = (1, 4, 16)
WIN_KEYS = 128
assert all(w // d == WIN_KEYS for w, d in zip(WINDOWS, DILATIONS))
PAST_LEN = 16384
ROPE_THETA = 10000.0
EPS = 1e-6
NEG = -1e30

C_Q, C_K, C_V, C_Z = 0, 512, 1024, 1536
C_AVAL, C_AGATE, C_ZA = 2048, 2304, 2560
C_CB, C_CC, C_CH, C_ZC = 2816, 3072, 3328, 3584
D_IN = 3840

LANES = 128
VMEM_LIMIT = 56 * 1024 * 1024

F32 = jnp.float32
BF16 = jnp.bfloat16


def _sigmoid(x):
    return 1.0 / (1.0 + jnp.exp(-x))


def _silu(x):
    return x * _sigmoid(x)


def _in_proj_kernel(x_ref, lng_ref, w_ref, cos_ref, sin_ref, qg_ref, kg_ref, seg_ref,
                    q_ref, k_ref, v_ref, kf_ref, vf_ref, zs_ref, u_ref, gza_ref, bz_ref, uc_ref):
    x = x_ref[...]
    ms = jnp.mean(x * x, axis=-1, keepdims=True)
    xn = (x * lax.rsqrt(ms + EPS) * lng_ref[...]).astype(BF16)

    def proj(c0, c1):
        return jnp.dot(xn, w_ref[:, c0:c1], preferred_element_type=F32)

    cos = cos_ref[...]
    sin = sin_ref[...]
    tm = x.shape[0]
    first_half = (lax.broadcasted_iota(jnp.int32, (tm, LANES), 1) & (HEAD_DIM // 2)) == 0

    def norm_rope(p, g_ref, scale, out_bf_ref, out_f_ref):
        for s in range(D_ATTN // LANES):
            sl = slice(s * LANES, (s + 1) * LANES)
            ps = p[:, sl]
            ss = jnp.dot((ps * ps).astype(BF16), seg_ref[...], preferred_element_type=F32)
            pn = ps * lax.rsqrt(ss + EPS) * g_ref[...]
            partner = jnp.where(first_half,
                                pltpu.roll(pn, LANES - HEAD_DIM // 2, 1),
                                pltpu.roll(pn, HEAD_DIM // 2, 1))
            r = pn * cos + partner * sin
            if out_f_ref is not None:
                out_f_ref[:, sl] = r
            out_bf_ref[:, sl] = (r * scale).astype(BF16)

    norm_rope(proj(C_Q, C_K), qg_ref, HEAD_DIM ** -0.5, q_ref, None)
    norm_rope(proj(C_K, C_V), kg_ref, 1.0, k_ref, kf_ref)
    v = proj(C_V, C_Z)
    vf_ref[...] = v
    v_ref[...] = v.astype(BF16)
    zs_ref[...] = _silu(proj(C_Z, C_AVAL))
    ag = proj(C_AVAL, C_ZA)
    u_ref[...] = ag[:, :D_CONV] * _sigmoid(ag[:, D_CONV:])
    zb = proj(C_ZA, C_CC)
    gza_ref[...] = _silu(zb[:, :D_CONV])
    zc = proj(C_ZC, D_IN)
    bz_ref[...] = zb[:, D_CONV:] * _silu(zc)
    ch = proj(C_CC, C_ZC)
    uc_ref[...] = ch[:, :D_SCONV] * ch[:, D_SCONV:]


def _in_proj(x2d, lng, w_bf, cos_t, sin_t, qg, kg, seg, tm):
    n = x2d.shape[0]
    n_pos_blocks = cos_t.shape[0] // tm
    row = lambda c: pl.BlockSpec((tm, c), lambda i: (i, 0))
    full = lambda a: pl.BlockSpec(a.shape, lambda i: (0,) * a.ndim)
    pos = pl.BlockSpec((tm, LANES), lambda i: (i % n_pos_blocks, 0))
    out_shapes = (
        jax.ShapeDtypeStruct((n, D_ATTN), BF16),
        jax.ShapeDtypeStruct((n, D_ATTN), BF16),
        jax.ShapeDtypeStruct((n, D_ATTN), BF16),
        jax.ShapeDtypeStruct((n, D_ATTN), F32),
        jax.ShapeDtypeStruct((n, D_ATTN), F32),
        jax.ShapeDtypeStruct((n, D_ATTN), F32),
        jax.ShapeDtypeStruct((n, D_CONV), F32),
        jax.ShapeDtypeStruct((n, D_CONV), F32),
        jax.ShapeDtypeStruct((n, D_SCONV), F32),
        jax.ShapeDtypeStruct((n, D_SCONV), F32),
    )
    return pl.pallas_call(
        _in_proj_kernel,
        grid=(n // tm,),
        in_specs=[row(D_MODEL), full(lng), full(w_bf), pos, pos, full(qg), full(kg), full(seg)],
        out_specs=tuple(row(s.shape[1]) for s in out_shapes),
        out_shape=out_shapes,
        compiler_params=pltpu.CompilerParams(dimension_semantics=("arbitrary",),
                                             vmem_limit_bytes=VMEM_LIMIT),
        name="in_proj",
    )(x2d, lng, w_bf, cos_t, sin_t, qg, kg, seg)


def _win_attn_kernel(q_ref, k_ref, v_ref, o_ref, lse_ref, *, tq):
    qi = pl.program_id(2)
    blk = WIN_KEYS
    lane = lax.broadcasted_iota(jnp.int32, (1, LANES), 1)
    head_a = lane < HEAD_DIM
    rel0 = (lax.broadcasted_iota(jnp.int32, (blk, 2 * blk), 1)
            - lax.broadcasted_iota(jnp.int32, (blk, 2 * blk), 0))
    for j in range(tq // blk):
        q0 = qi * tq + j * blk
        kk = pl.multiple_of(jnp.maximum(q0 - blk, 0), blk)
        rel = rel0 - (q0 - kk)
        valid = (rel <= 0) & (rel >= -WIN_KEYS)
        rows = slice(j * blk, (j + 1) * blk)
        qb = q_ref[rows, :]
        kb = k_ref[pl.ds(kk, 2 * blk), :]
        vb = v_ref[pl.ds(kk, 2 * blk), :]
        res, mx = [], []
        for sel in (head_a, jnp.logical_not(head_a)):
            qh = jnp.where(sel, qb, jnp.zeros_like(qb))
            s = lax.dot_general(qh, kb, (((1,), (1,)), ((), ())), preferred_element_type=F32)
            s = jnp.where(valid, s, NEG)
            m = jnp.max(s, axis=1, keepdims=True)
            p = jnp.exp(s - m).astype(BF16)
            vh = jnp.where(sel, vb, jnp.ones_like(vb))
            res.append(jnp.dot(p, vh, preferred_element_type=F32))
            mx.append(m)
        num = jnp.where(head_a, res[0], res[1])
        den = pltpu.roll(jnp.where(head_a, res[1], res[0]), HEAD_DIM, 1)
        o_ref[rows, :] = num / den
        lse_ref[rows, :] = jnp.where(head_a, mx[0], mx[1]) + jnp.log(den)


def _win_attn(q, k, v, d, tq):
    b, s, _ = q.shape
    sd = s // d
    tq = min(tq, sd)
    view = lambda a: a.reshape(b, sd, d * D_ATTN)
    n_col = d * D_ATTN // LANES
    qspec = pl.BlockSpec((None, tq, LANES), lambda bi, c, i: (bi, i, c))
    kvspec = pl.BlockSpec((None, sd, LANES), lambda bi, c, i: (bi, 0, c))
    o, lse = pl.pallas_call(
        functools.partial(_win_attn_kernel, tq=tq),
        grid=(b, n_col, sd // tq),
        in_specs=[qspec, kvspec, kvspec],
        out_specs=(qspec, qspec),
        out_shape=(jax.ShapeDtypeStruct((b, sd, d * D_ATTN), F32),) * 2,
        compiler_params=pltpu.CompilerParams(
            dimension_semantics=("arbitrary", "arbitrary", "arbitrary"),
            vmem_limit_bytes=VMEM_LIMIT),
        name=f"win_attn_d{d}",
    )(view(q), view(k), view(v))
    return o.reshape(b, s, D_ATTN), lse.reshape(b, s, D_ATTN)


A_HALO = 32
C_HALO = 8


def _layernorm_rows(x, g, b):
    mu = jnp.mean(x, axis=-1, keepdims=True)
    xc = x - mu
    var = jnp.mean(xc * xc, axis=-1, keepdims=True)
    return xc * lax.rsqrt(var + EPS) * g + b


def _out_proj_kernel(x_ref, o1_ref, l1_ref, o4_ref, l4_ref, o16_ref, l16_ref, zs_ref,
                     u_ref, uh_ref, gza_ref, bz_ref, uc_ref, uch_ref,
                     aw_ref, ab_ref, ag_ref, alb_ref, cw_ref, w_ref,
                     y_ref, ubuf, ucbuf, *, tm):
    first_tile = pl.program_id(1) == 0

    l1, l4, l16 = l1_ref[...], l4_ref[...], l16_ref[...]
    mx = jnp.maximum(jnp.maximum(l1, l4), l16)
    w1, w4, w16 = jnp.exp(l1 - mx), jnp.exp(l4 - mx), jnp.exp(l16 - mx)
    attn = (w1 * o1_ref[...] + w4 * o4_ref[...] + w16 * o16_ref[...]) / (w1 + w4 + w16)
    acc = jnp.dot((attn * zs_ref[...]).astype(BF16), w_ref[0:D_ATTN, :],
                  preferred_element_type=F32)

    ubuf[0:A_HALO, :] = jnp.where(first_tile, 0.0, uh_ref[...])
    ubuf[A_HALO:A_HALO + tm, :] = u_ref[...]
    base = A_HALO - (CONV_WIDTH - 1)
    ca = jnp.zeros((tm, D_CONV), F32) + ab_ref[...]
    for j in range(CONV_WIDTH):
        ca = ca + aw_ref[j:j + 1, :] * ubuf[base + j:base + j + tm, :]
    ya = _silu(_layernorm_rows(ca, ag_ref[...], alb_ref[...])) * gza_ref[...]
    acc = acc + jnp.dot(ya.astype(BF16), w_ref[D_ATTN:D_ATTN + D_CONV, :],
                        preferred_element_type=F32)

    ucbuf[0:C_HALO, :] = jnp.where(first_tile, 0.0, uch_ref[...])
    ucbuf[C_HALO:C_HALO + tm, :] = uc_ref[...]
    cbase = C_HALO - (SCONV_WIDTH - 1)
    cc = jnp.zeros((tm, D_SCONV), F32)
    for j in range(SCONV_WIDTH):
        cc = cc + cw_ref[j:j + 1, :] * ucbuf[cbase + j:cbase + j + tm, :]
    yc = bz_ref[...] * cc
    acc = acc + jnp.dot(yc.astype(BF16), w_ref[D_ATTN + D_CONV:, :],
                        preferred_element_type=F32)

    y_ref[...] = x_ref[...] + acc


def _out_proj(x, parts, zs, u, gza, bz, uc, aw, ab, ag, alb, cw, w_bf, tm):
    b, s, _ = x.shape
    tile = lambda c: pl.BlockSpec((None, tm, c), lambda bi, i: (bi, i, 0))
    halo = lambda h, c: pl.BlockSpec(
        (None, h, c), lambda bi, i: (bi, jnp.maximum(i * (tm // h) - 1, 0), 0))
    full = lambda a: pl.BlockSpec(a.shape, lambda bi, i: (0,) * a.ndim)
    (o1, l1), (o4, l4), (o16, l16) = parts
    return pl.pallas_call(
        functools.partial(_out_proj_kernel, tm=tm),
        grid=(b, s // tm),
        in_specs=[tile(D_MODEL)] + [tile(D_ATTN)] * 7
                 + [tile(D_CONV), halo(A_HALO, D_CONV), tile(D_CONV), tile(D_SCONV),
                    tile(D_SCONV), halo(C_HALO, D_SCONV)]
                 + [full(aw), full(ab), full(ag), full(alb), full(cw), full(w_bf)],
        out_specs=tile(D_MODEL),
        out_shape=jax.ShapeDtypeStruct((b, s, D_MODEL), F32),
        scratch_shapes=[pltpu.VMEM((A_HALO + tm, D_CONV), F32),
                        pltpu.VMEM((C_HALO + tm, D_SCONV), F32)],
        compiler_params=pltpu.CompilerParams(dimension_semantics=("arbitrary", "arbitrary"),
                                             vmem_limit_bytes=VMEM_LIMIT),
        name="out_proj",
    )(x, o1, l1, o4, l4, o16, l16, zs, u, u, gza, bz, uc, uc, aw, ab, ag, alb, cw, w_bf)


def _sample_attn_kernel(q_ref, kn_ref, vn_ref, k1_ref, k4_ref, k16_ref, v1_ref, v4_ref, v16_ref,
                        attn_ref):
    kn = kn_ref[...]
    vn = vn_ref[...]
    head_of_lane = lax.broadcasted_iota(jnp.int32, (N_HEADS, D_ATTN), 1) // HEAD_DIM
    own = head_of_lane == lax.broadcasted_iota(jnp.int32, (N_HEADS, D_ATTN), 0)
    qblk = jnp.where(own, q_ref[...], 0.0)
    qb = qblk.astype(BF16)
    s_self = jnp.sum(qblk * kn, axis=1, keepdims=True)
    scores = [lax.dot_general(qb, kr[...].astype(BF16), (((1,), (1,)), ((), ())),
                              preferred_element_type=F32) for kr in (k1_ref, k4_ref, k16_ref)]
    m = s_self
    for s in scores:
        m = jnp.maximum(m, jnp.max(s, axis=1, keepdims=True))
    e_self = len(WINDOWS) * jnp.exp(s_self - m)
    den = e_self
    pv = jnp.zeros((N_HEADS, D_ATTN), F32)
    for s, vr in zip(scores, (v1_ref, v4_ref, v16_ref)):
        e = jnp.exp(s - m).astype(BF16)
        den = den + jnp.sum(e.astype(F32), axis=1, keepdims=True)
        pv = pv + jnp.dot(e, vr[...].astype(BF16), preferred_element_type=F32)
    per_lane = lambda a: jnp.sum(jnp.where(own, a, 0.0), axis=0, keepdims=True)
    num = per_lane(pv) + per_lane(jnp.broadcast_to(e_self, own.shape)) * vn
    attn_ref[...] = num / per_lane(jnp.broadcast_to(den, own.shape))


def _sample_attn(q, kf, vf, cache_k, cache_v):
    bd, l_cache, _ = cache_k.shape
    assert l_cache == WINDOWS[-1] and all(l_cache % d == 0 for d in DILATIONS)
    row = pl.BlockSpec((None, 1, D_ATTN), lambda b: (b, 0, 0))
    r3 = lambda a: a.reshape(bd, 1, D_ATTN)
    views, specs = [], []
    for cache in (cache_k, cache_v):
        for d in DILATIONS:
            last = l_cache // d // WIN_KEYS - 1
            views.append(cache.reshape(bd, l_cache // d, d * D_ATTN))
            specs.append(pl.BlockSpec((None, WIN_KEYS, D_ATTN), lambda b, last=last: (b, last, 0)))
    attn = pl.pallas_call(
        _sample_attn_kernel,
        grid=(bd,),
        in_specs=[row, row, row] + specs,
        out_specs=row,
        out_shape=jax.ShapeDtypeStruct((bd, 1, D_ATTN), F32),
        compiler_params=pltpu.CompilerParams(dimension_semantics=("arbitrary",),
                                             vmem_limit_bytes=VMEM_LIMIT),
        name="sample_attn",
    )(r3(q), r3(kf), r3(vf), *views)
    return attn.reshape(bd, D_ATTN)


def _cache_shift_kernel(ck_ref, cv_ref, kn_ref, vn_ref, nk_ref, nv_ref):
    l_cache = ck_ref.shape[0]
    nk_ref[...] = pltpu.roll(ck_ref[...], l_cache - 1, 0)
    nk_ref[l_cache - 1:l_cache, :] = kn_ref[...]
    nv_ref[...] = pltpu.roll(cv_ref[...], l_cache - 1, 0)
    nv_ref[l_cache - 1:l_cache, :] = vn_ref[...]


def _cache_shift(cache_k, cache_v, k_new, v_new):
    depth, bd, l_cache, _ = cache_k.shape
    cache = pl.BlockSpec((None, None, l_cache, D_ATTN), lambda l, b: (l, b, 0, 0))
    row = pl.BlockSpec((None, None, 1, D_ATTN), lambda l, b: (l, b, 0, 0))
    return pl.pallas_call(
        _cache_shift_kernel,
        grid=(depth, bd),
        in_specs=[cache, cache, row, row],
        out_specs=(cache, cache),
        out_shape=(jax.ShapeDtypeStruct(cache_k.shape, F32),) * 2,
        compiler_params=pltpu.CompilerParams(dimension_semantics=("arbitrary", "arbitrary"),
                                             vmem_limit_bytes=VMEM_LIMIT),
        name="cache_shift",
    )(cache_k, cache_v, k_new, v_new)


def _sample_out_kernel(x_ref, attn_ref, zs_ref, u_ref, gza_ref, bz_ref, uc_ref, sa_ref, sc_ref,
                       aw_ref, ab_ref, ag_ref, alb_ref, cw_ref, w_ref, y_ref, na_ref, nc_ref):
    u = u_ref[...]
    uc = uc_ref[...]
    na = CONV_WIDTH - 1
    nc = SCONV_WIDTH - 1
    ca = ab_ref[...] + aw_ref[na:na + 1, :] * u
    for j in range(na):
        ca = ca + aw_ref[j:j + 1, :] * sa_ref[:, j * D_CONV:(j + 1) * D_CONV]
    ya = _silu(_layernorm_rows(ca, ag_ref[...], alb_ref[...])) * gza_ref[...]
    cc = cw_ref[nc:nc + 1, :] * uc
    for j in range(nc):
        cc = cc + cw_ref[j:j + 1, :] * sc_ref[:, j * D_SCONV:(j + 1) * D_SCONV]
    yc = bz_ref[...] * cc
    acc = jnp.dot((attn_ref[...] * zs_ref[...]).astype(BF16), w_ref[0:D_ATTN, :],
                  preferred_element_type=F32)
    acc = acc + jnp.dot(ya.astype(BF16), w_ref[D_ATTN:D_ATTN + D_CONV, :],
                        preferred_element_type=F32)
    acc = acc + jnp.dot(yc.astype(BF16), w_ref[D_ATTN + D_CONV:, :],
                        preferred_element_type=F32)
    y_ref[...] = x_ref[...] + acc
    na_ref[:, 0:(na - 1) * D_CONV] = sa_ref[:, D_CONV:na * D_CONV]
    na_ref[:, (na - 1) * D_CONV:] = u
    nc_ref[:, 0:(nc - 1) * D_SCONV] = sc_ref[:, D_SCONV:nc * D_SCONV]
    nc_ref[:, (nc - 1) * D_SCONV:] = uc


def _sample_out(x, attn, zs, u, gza, bz, uc, sa, sc, aw, ab, ag, alb, cw, w_bf):
    bd = x.shape[0]
    sa2 = sa.reshape(bd, -1)
    sc2 = sc.reshape(bd, -1)
    args = (x, attn, zs, u, gza, bz, uc, sa2, sc2, aw, ab, ag, alb, cw, w_bf)
    full = lambda a: pl.BlockSpec(a.shape, lambda i: (0,) * a.ndim)
    y, na, nc = pl.pallas_call(
        _sample_out_kernel,
        grid=(1,),
        in_specs=[full(a) for a in args],
        out_specs=(full(x), full(sa2), full(sc2)),
        out_shape=(jax.ShapeDtypeStruct(x.shape, F32),
                   jax.ShapeDtypeStruct(sa2.shape, F32),
                   jax.ShapeDtypeStruct(sc2.shape, F32)),
        compiler_params=pltpu.CompilerParams(dimension_semantics=("arbitrary",),
                                             vmem_limit_bytes=VMEM_LIMIT),
        name="sample_out",
    )(*args)
    return y, na.reshape(sa.shape), nc.reshape(sc.shape)


def _rope_tables(pos):
    half = HEAD_DIM // 2
    inv = ROPE_THETA ** (-jnp.arange(half, dtype=F32) / half)
    ang = pos.astype(F32)[:, None] * inv[None, :]
    cos, sin = jnp.cos(ang), jnp.sin(ang)
    reps = LANES // HEAD_DIM
    cos_t = jnp.tile(jnp.concatenate([cos, cos], axis=-1), (1, reps))
    sin_t = jnp.tile(jnp.concatenate([-sin, sin], axis=-1), (1, reps))
    return cos_t, sin_t


def _segment_mean_matrix():
    seg = np.kron(np.eye(LANES // HEAD_DIM), np.ones((HEAD_DIM, HEAD_DIM))) / HEAD_DIM
    return jnp.asarray(seg, dtype=BF16)


PROMPT_TM_IN = 512
PROMPT_TM_OUT = 512
PROMPT_TQ = 512


def kernel(x_prompt, x_sample, cache_k, cache_v, state_conv_a, state_conv_c, ln_g, w_in,
           q_norm_g, k_norm_g, a_conv_w, a_conv_b, a_ln_g, a_ln_b, c_conv_w, w_out):
    bp, sp, _ = x_prompt.shape
    bd, ts, _ = x_sample.shape
    assert ts == 1
    depth = w_in.shape[0]
    l_cache = cache_k.shape[2]
    l_prompt = min(WINDOWS[-1], sp)

    cos_p, sin_p = _rope_tables(jnp.arange(sp, dtype=jnp.int32))
    cos_s, sin_s = _rope_tables(jnp.full((bd,), PAST_LEN, dtype=jnp.int32))
    seg = _segment_mean_matrix()
    row = lambda a: a.reshape(1, -1)
    tile_g = lambda g: jnp.tile(g, LANES // HEAD_DIM).reshape(1, LANES)

    hp = x_prompt
    hs = x_sample.reshape(bd, D_MODEL)
    pk, pv, pa, pc, sk, sv, sa, sc = [], [], [], [], [], [], [], []
    for l in range(depth):
        w_in_bf = w_in[l].astype(BF16)
        w_out_bf = w_out[l].astype(BF16)
        lng, qg, kg = row(ln_g[l]), tile_g(q_norm_g[l]), tile_g(k_norm_g[l])
        conv_w = (a_conv_w[l], row(a_conv_b[l]), row(a_ln_g[l]), row(a_ln_b[l]), c_conv_w[l])

        q, k, v, kf, vf, zs, u, gza, bz, uc = _in_proj(
            hp.reshape(bp * sp, D_MODEL), lng, w_in_bf, cos_p, sin_p, qg, kg, seg, PROMPT_TM_IN)
        sh = lambda a: a.reshape(bp, sp, a.shape[-1])
        parts = [_win_attn(sh(q), sh(k), sh(v), d, PROMPT_TQ) for d in DILATIONS]
        hp = _out_proj(hp, parts, sh(zs), sh(u), sh(gza), sh(bz), sh(uc), *conv_w, w_out_bf,
                       PROMPT_TM_OUT)
        pk.append(sh(kf)[:, sp - l_prompt:].reshape(bp, l_prompt, N_HEADS, HEAD_DIM))
        pv.append(sh(vf)[:, sp - l_prompt:].reshape(bp, l_prompt, N_HEADS, HEAD_DIM))
        pa.append(sh(u)[:, sp - (CONV_WIDTH - 1):])
        pc.append(sh(uc)[:, sp - (SCONV_WIDTH - 1):])

        q, k, v, kf, vf, zs, u, gza, bz, uc = _in_proj(
            hs, lng, w_in_bf, cos_s, sin_s, qg, kg, seg, bd)
        attn = _sample_attn(q.astype(F32), kf, vf, cache_k[l].reshape(bd, l_cache, D_ATTN),
                            cache_v[l].reshape(bd, l_cache, D_ATTN))
        hs, na, nc = _sample_out(hs, attn, zs, u, gza, bz, uc, state_conv_a[l], state_conv_c[l],
                                 *conv_w, w_out_bf)
        sk.append(kf.reshape(bd, 1, D_ATTN))
        sv.append(vf.reshape(bd, 1, D_ATTN))
        sa.append(na)
        sc.append(nc)

    cache_shape = cache_k.shape
    nk, nv = _cache_shift(cache_k.reshape(depth, bd, l_cache, D_ATTN),
                          cache_v.reshape(depth, bd, l_cache, D_ATTN),
                          jnp.stack(sk), jnp.stack(sv))
    return (hp, hs.reshape(bd, ts, D_MODEL), jnp.stack(pk), jnp.stack(pv), jnp.stack(pa),
            jnp.stack(pc), nk.reshape(cache_shape), nv.reshape(cache_shape),
            jnp.stack(sa), jnp.stack(sc))
```

```python
import functools

import numpy as np
import jax
import jax.numpy as jnp
from jax import lax
from jax.experimental import pallas as pl
from jax.experimental.pallas import tpu as pltpu

D_MODEL = 1024
N_HEADS = 8
HEAD_DIM = 64
D_ATTN = N_HEADS * HEAD_DIM
D_CONV = 256
CONV_WIDTH = 31
D_SCONV = 256
SCONV_WIDTH = 3
WINDOWS = (128, 512, 2048)
DILATIONS = (1, 4, 16)
WIN_KEYS = 128
assert all(w // d == WIN_KEYS for w, d in zip(WINDOWS, DILATIONS))
assert all(d & (d - 1) == 0 for d in DILATIONS)
PAST_LEN = 16384
ROPE_THETA = 10000.0
EPS = 1e-6
NEG = -1e30

C_Q, C_K, C_V, C_Z = 0, 512, 1024, 1536
C_AVAL, C_AGATE, C_ZA = 2048, 2304, 2560
C_CB, C_CC, C_CH, C_ZC = 2816, 3072, 3328, 3584
D_IN = 3840

LANES = 128
N_SLABS = D_ATTN // LANES
VMEM_LIMIT = 56 * 1024 * 1024

F32 = jnp.float32
BF16 = jnp.bfloat16


def _sigmoid(x):
    return 1.0 / (1.0 + jnp.exp(-x))


def _silu(x):
    return x * _sigmoid(x)


def _in_proj_body(x_ref, lng_ref, w_ref, cos_ref, sin_ref, qg_ref, kg_ref, seg_ref,
                  zs_ref, u_ref, gza_ref, bz_ref, uc_ref, emit_q, emit_k, emit_v):
    x = x_ref[...]
    ms = jnp.mean(x * x, axis=-1, keepdims=True)
    xn = (x * lax.rsqrt(ms + EPS) * lng_ref[...]).astype(BF16)

    def proj(c0, c1):
        return jnp.dot(xn, w_ref[:, c0:c1], preferred_element_type=F32)

    cos = cos_ref[...]
    sin = sin_ref[...]
    tm = x.shape[0]
    first_half = (lax.broadcasted_iota(jnp.int32, (tm, LANES), 1) & (HEAD_DIM // 2)) == 0

    def norm_rope(p, g_ref, scale, emit):
        for s in range(N_SLABS):
            ps = p[:, s * LANES:(s + 1) * LANES]
            ss = jnp.dot((ps * ps).astype(BF16), seg_ref[...], preferred_element_type=F32)
            pn = ps * lax.rsqrt(ss + EPS) * g_ref[...]
            partner = jnp.where(first_half,
                                pltpu.roll(pn, LANES - HEAD_DIM // 2, 1),
                                pltpu.roll(pn, HEAD_DIM // 2, 1))
            emit(s, (pn * cos + partner * sin) * scale)

    norm_rope(proj(C_Q, C_K), qg_ref, HEAD_DIM ** -0.5, emit_q)
    norm_rope(proj(C_K, C_V), kg_ref, 1.0, emit_k)
    v = proj(C_V, C_Z)
    for s in range(N_SLABS):
        emit_v(s, v[:, s * LANES:(s + 1) * LANES])
    zs_ref[...] = _silu(proj(C_Z, C_AVAL))
    ag = proj(C_AVAL, C_ZA)
    u_ref[...] = ag[:, :D_CONV] * _sigmoid(ag[:, D_CONV:])
    zb = proj(C_ZA, C_CC)
    gza_ref[...] = _silu(zb[:, :D_CONV])
    zc = proj(C_ZC, D_IN)
    bz_ref[...] = zb[:, D_CONV:] * _silu(zc)
    ch = proj(C_CC, C_ZC)
    uc_ref[...] = ch[:, :D_SCONV] * ch[:, D_SCONV:]


def _in_proj_prompt_kernel(x_ref, lng_ref, w_ref, cos_ref, sin_ref, qg_ref, kg_ref, seg_ref,
                           q1_ref, q4_ref, q16_ref, k1_ref, k4_ref, k16_ref,
                           v1_ref, v4_ref, v16_ref, kt_ref, vt_ref,
                           zs_ref, u_ref, gza_ref, bz_ref, uc_ref, stage_ref, *, tm, n_skip):
    keep_tail = pl.program_id(1) >= n_skip

    def emitter(which, outs, t_ref):
        def emit(s, val):
            stage_ref[which, s] = val
            for d, out in zip(DILATIONS, outs):
                if d == 1:
                    out[s, 0] = val.astype(BF16)
                else:
                    for r in range(d):
                        out[s, r] = stage_ref[which, s, pl.ds(r, tm // d, stride=d), :].astype(BF16)
            if t_ref is not None:
                @pl.when(keep_tail)
                def _():
                    t_ref[s * LANES:(s + 1) * LANES, :] = val.T

                @pl.when(jnp.logical_not(keep_tail))
                def _():
                    t_ref[s * LANES:(s + 1) * LANES, :] = jnp.zeros((LANES, tm), F32)
        return emit

    _in_proj_body(x_ref, lng_ref, w_ref, cos_ref, sin_ref, qg_ref, kg_ref, seg_ref,
                  zs_ref, u_ref, gza_ref, bz_ref, uc_ref,
                  emitter(0, (q1_ref, q4_ref, q16_ref), None),
                  emitter(1, (k1_ref, k4_ref, k16_ref), kt_ref),
                  emitter(2, (v1_ref, v4_ref, v16_ref), vt_ref))


def _in_proj_prompt(x, lng, w_bf, cos_t, sin_t, qg, kg, seg, tm, l_tail):
    b, s, _ = x.shape
    n_skip = (s - l_tail) // tm
    tile = lambda c: pl.BlockSpec((None, tm, c), lambda bi, i: (bi, i, 0))
    full = lambda a: pl.BlockSpec(a.shape, lambda bi, i: (0,) * a.ndim)
    pos = pl.BlockSpec((tm, LANES), lambda bi, i: (i, 0))
    deint = lambda d: pl.BlockSpec((None, N_SLABS, d, tm // d, LANES),
                                   lambda bi, i: (bi, 0, 0, i, 0))
    tail = pl.BlockSpec((None, D_ATTN, tm), lambda bi, i: (bi, 0, jnp.maximum(i - n_skip, 0)))
    deint_shape = lambda d: jax.ShapeDtypeStruct((b, N_SLABS, d, s // d, LANES), BF16)
    out_shapes = tuple(deint_shape(d) for _ in range(3) for d in DILATIONS) + (
        jax.ShapeDtypeStruct((b, D_ATTN, l_tail), F32),
        jax.ShapeDtypeStruct((b, D_ATTN, l_tail), F32),
        jax.ShapeDtypeStruct((b, s, D_ATTN), F32),
        jax.ShapeDtypeStruct((b, s, D_CONV), F32),
        jax.ShapeDtypeStruct((b, s, D_CONV), F32),
        jax.ShapeDtypeStruct((b, s, D_SCONV), F32),
        jax.ShapeDtypeStruct((b, s, D_SCONV), F32),
    )
    out_specs = tuple(deint(d) for _ in range(3) for d in DILATIONS) + (
        tail, tail, tile(D_ATTN), tile(D_CONV), tile(D_CONV), tile(D_SCONV), tile(D_SCONV))
    return pl.pallas_call(
        functools.partial(_in_proj_prompt_kernel, tm=tm, n_skip=n_skip),
        grid=(b, s // tm),
        in_specs=[tile(D_MODEL), full(lng), full(w_bf), pos, pos, full(qg), full(kg), full(seg)],
        out_specs=out_specs,
        out_shape=out_shapes,
        scratch_shapes=[pltpu.VMEM((3, N_SLABS, tm, LANES), F32)],
        compiler_params=pltpu.CompilerParams(dimension_semantics=("arbitrary", "arbitrary"),
                                             vmem_limit_bytes=VMEM_LIMIT),
        name="in_proj",
    )(x, lng, w_bf, cos_t, sin_t, qg, kg, seg)


def _in_proj_sample_kernel(x_ref, lng_ref, w_ref, cos_ref, sin_ref, qg_ref, kg_ref, seg_ref,
                           q_ref, k_ref, v_ref, zs_ref, u_ref, gza_ref, bz_ref, uc_ref):
    def emitter(out):
        def emit(s, val):
            out[:, s * LANES:(s + 1) * LANES] = val
        return emit

    _in_proj_body(x_ref, lng_ref, w_ref, cos_ref, sin_ref, qg_ref, kg_ref, seg_ref,
                  zs_ref, u_ref, gza_ref, bz_ref, uc_ref,
                  emitter(q_ref), emitter(k_ref), emitter(v_ref))


def _in_proj_sample(x2d, lng, w_bf, cos_t, sin_t, qg, kg, seg):
    n = x2d.shape[0]
    args = (x2d, lng, w_bf, cos_t, sin_t, qg, kg, seg)
    full = lambda a: pl.BlockSpec(a.shape, lambda i: (0,) * a.ndim)
    widths = (D_ATTN, D_ATTN, D_ATTN, D_ATTN, D_CONV, D_CONV, D_SCONV, D_SCONV)
    out_shapes = tuple(jax.ShapeDtypeStruct((n, c), F32) for c in widths)
    return pl.pallas_call(
        _in_proj_sample_kernel,
        grid=(1,),
        in_specs=[full(a) for a in args],
        out_specs=tuple(full(s) for s in out_shapes),
        out_shape=out_shapes,
        compiler_params=pltpu.CompilerParams(dimension_semantics=("arbitrary",),
                                             vmem_limit_bytes=VMEM_LIMIT),
        name="in_proj_sample",
    )(*args)


ATTN_TILE = 2048
assert all(ATTN_TILE % (WIN_KEYS * d) == 0 for d in DILATIONS)
ATTN_UNROLL = 4
assert (ATTN_TILE // WIN_KEYS) % ATTN_UNROLL == 0


def _attn_kernel(*refs, tile):
    n_pat = len(DILATIONS)
    qkv_refs = [refs[3 * p:3 * p + 3] for p in range(n_pat)]
    bias_ref, o_ref, acc_ref, m_ref, l_ref = refs[3 * n_pat:]
    t = pl.program_id(2)
    blk = WIN_KEYS
    n_blocks = tile // blk
    lane = lax.broadcasted_iota(jnp.int32, (1, LANES), 1)
    head_a = lane < HEAD_DIM

    def block_softmax(qb, k_ref, v_ref, r, q0):
        kk = pl.multiple_of(jnp.maximum(q0 - blk, 0), blk)
        bias = bias_ref[jnp.minimum(q0, 1)]
        kb = k_ref[r, pl.ds(kk, 2 * blk), :]
        vb = v_ref[r, pl.ds(kk, 2 * blk), :]
        res, mx = [], []
        for sel in (head_a, jnp.logical_not(head_a)):
            qh = jnp.where(sel, qb, jnp.zeros_like(qb))
            s = lax.dot_general(qh, kb, (((1,), (1,)), ((), ())), preferred_element_type=F32)
            s = s + bias
            m = jnp.max(s, axis=1, keepdims=True)
            p = jnp.exp(s - m).astype(BF16)
            vh = jnp.where(sel, vb, jnp.ones_like(vb))
            res.append(jnp.dot(p, vh, preferred_element_type=F32))
            mx.append(m)
        num = jnp.where(head_a, res[0], res[1])
        den = pltpu.roll(jnp.where(head_a, res[1], res[0]), HEAD_DIM, 1)
        return num, den, jnp.where(head_a, mx[0], mx[1])

    for p, d in enumerate(DILATIONS):
        q_ref, k_ref, v_ref = qkv_refs[p]
        per_res = n_blocks // d
        assert per_res & (per_res - 1) == 0
        first, last = p == 0, p == n_pat - 1

        def body(it, carry, d=d, q_ref=q_ref, k_ref=k_ref, v_ref=v_ref, per_res=per_res,
                 first=first, last=last):
            blocks = []
            for u in range(ATTN_UNROLL):
                i = it * ATTN_UNROLL + u
                r = lax.shift_right_logical(i, per_res.bit_length() - 1)
                jb = i & (per_res - 1)
                qb = q_ref[r, pl.ds(pl.multiple_of(jb * blk, blk), blk), :]
                num, den, mb = block_softmax(qb, k_ref, v_ref, r, t * (tile // d) + jb * blk)
                start = jb * blk * d + r
                rows = pl.ds(start, blk) if d == 1 else pl.ds(start, blk, stride=d)
                blocks.append((rows, num, den, mb))
            if not first:
                old = [(m_ref[rows, :], acc_ref[rows, :], l_ref[rows, :]) for rows, *_ in blocks]
                merged = []
                for (rows, num, den, mb), (m_old, acc_old, l_old) in zip(blocks, old):
                    m_new = jnp.maximum(m_old, mb)
                    w_old = jnp.exp(m_old - m_new)
                    w_blk = jnp.exp(mb - m_new)
                    merged.append((rows, acc_old * w_old + num * w_blk,
                                   l_old * w_old + den * w_blk, m_new))
                blocks = merged
            for rows, num, den, mb in blocks:
                if last:
                    o_ref[rows, :] = num / den
                else:
                    acc_ref[rows, :] = num
                    l_ref[rows, :] = den
                    m_ref[rows, :] = mb
            return carry

        lax.fori_loop(0, n_blocks // ATTN_UNROLL, body, 0)


def _band_bias():
    rel = np.arange(2 * WIN_KEYS)[None, :] - np.arange(WIN_KEYS)[:, None]
    variants = [rel - off for off in (0, WIN_KEYS)]
    return jnp.asarray(np.stack([np.where((v <= 0) & (v >= -WIN_KEYS), 0.0, NEG)
                                 for v in variants]), dtype=F32)


def _attention(qkv, b, s):
    tile = ATTN_TILE
    in_specs, args = [], []
    for d, (q, k, v) in zip(DILATIONS, qkv):
        qspec = pl.BlockSpec((None, None, d, tile // d, LANES), lambda bi, c, i: (bi, c, 0, i, 0))
        kvspec = pl.BlockSpec((None, None, d, s // d, LANES), lambda bi, c, i: (bi, c, 0, 0, 0))
        in_specs += [qspec, kvspec, kvspec]
        args += [q, k, v]
    bias = _band_bias()
    in_specs.append(pl.BlockSpec(bias.shape, lambda bi, c, i: (0, 0, 0)))
    args.append(bias)
    return pl.pallas_call(
        functools.partial(_attn_kernel, tile=tile),
        grid=(b, N_SLABS, s // tile),
        in_specs=in_specs,
        out_specs=pl.BlockSpec((None, tile, LANES), lambda bi, c, i: (bi, i, c)),
        out_shape=jax.ShapeDtypeStruct((b, s, D_ATTN), F32),
        scratch_shapes=[pltpu.VMEM((tile, LANES), F32)] * 3,
        compiler_params=pltpu.CompilerParams(
            dimension_semantics=("arbitrary", "arbitrary", "arbitrary"),
            vmem_limit_bytes=VMEM_LIMIT),
        name="dilated_attn",
    )(*args)


A_HALO = 32
C_HALO = 8


def _layernorm_rows(x, g, b):
    mu = jnp.mean(x, axis=-1, keepdims=True)
    xc = x - mu
    var = jnp.mean(xc * xc, axis=-1, keepdims=True)
    return xc * lax.rsqrt(var + EPS) * g + b


def _out_proj_kernel(x_ref, attn_ref, zs_ref, u_ref, uh_ref, gza_ref, bz_ref, uc_ref, uch_ref,
                     aw_ref, ab_ref, ag_ref, alb_ref, cw_ref, w_ref,
                     y_ref, ubuf, ucbuf, *, tm):
    first_tile = pl.program_id(1) == 0
    acc = jnp.dot((attn_ref[...] * zs_ref[...]).astype(BF16), w_ref[0:D_ATTN, :],
                  preferred_element_type=F32)

    ubuf[0:A_HALO, :] = jnp.where(first_tile, 0.0, uh_ref[...])
    ubuf[A_HALO:A_HALO + tm, :] = u_ref[...]
    base = A_HALO - (CONV_WIDTH - 1)
    ca = jnp.zeros((tm, D_CONV), F32) + ab_ref[...]
    for j in range(CONV_WIDTH):
        ca = ca + aw_ref[j:j + 1, :] * ubuf[base + j:base + j + tm, :]
    ya = _silu(_layernorm_rows(ca, ag_ref[...], alb_ref[...])) * gza_ref[...]
    acc = acc + jnp.dot(ya.astype(BF16), w_ref[D_ATTN:D_ATTN + D_CONV, :],
                        preferred_element_type=F32)

    ucbuf[0:C_HALO, :] = jnp.where(first_tile, 0.0, uch_ref[...])
    ucbuf[C_HALO:C_HALO + tm, :] = uc_ref[...]
    cbase = C_HALO - (SCONV_WIDTH - 1)
    cc = jnp.zeros((tm, D_SCONV), F32)
    for j in range(SCONV_WIDTH):
        cc = cc + cw_ref[j:j + 1, :] * ucbuf[cbase + j:cbase + j + tm, :]
    yc = bz_ref[...] * cc
    acc = acc + jnp.dot(yc.astype(BF16), w_ref[D_ATTN + D_CONV:, :],
                        preferred_element_type=F32)

    y_ref[...] = x_ref[...] + acc


def _out_proj(x, attn, zs, u, gza, bz, uc, aw, ab, ag, alb, cw, w_bf, tm):
    b, s, _ = x.shape
    tile = lambda c: pl.BlockSpec((None, tm, c), lambda bi, i: (bi, i, 0))
    halo = lambda h, c: pl.BlockSpec(
        (None, h, c), lambda bi, i: (bi, jnp.maximum(i * (tm // h) - 1, 0), 0))
    full = lambda a: pl.BlockSpec(a.shape, lambda bi, i: (0,) * a.ndim)
    return pl.pallas_call(
        functools.partial(_out_proj_kernel, tm=tm),
        grid=(b, s // tm),
        in_specs=[tile(D_MODEL), tile(D_ATTN), tile(D_ATTN),
                  tile(D_CONV), halo(A_HALO, D_CONV), tile(D_CONV), tile(D_SCONV),
                  tile(D_SCONV), halo(C_HALO, D_SCONV)]
                 + [full(aw), full(ab), full(ag), full(alb), full(cw), full(w_bf)],
        out_specs=tile(D_MODEL),
        out_shape=jax.ShapeDtypeStruct((b, s, D_MODEL), F32),
        scratch_shapes=[pltpu.VMEM((A_HALO + tm, D_CONV), F32),
                        pltpu.VMEM((C_HALO + tm, D_SCONV), F32)],
        compiler_params=pltpu.CompilerParams(dimension_semantics=("arbitrary", "arbitrary"),
                                             vmem_limit_bytes=VMEM_LIMIT),
        name="out_proj",
    )(x, attn, zs, u, u, gza, bz, uc, uc, aw, ab, ag, alb, cw, w_bf)


def _sample_attn_kernel(q_ref, kn_ref, vn_ref, kt_ref, vt_ref, attn_ref):
    l_cache = kt_ref.shape[1]
    kn = kn_ref[...]
    vn = vn_ref[...]
    head_of_lane = lax.broadcasted_iota(jnp.int32, (N_HEADS, D_ATTN), 1) // HEAD_DIM
    own = head_of_lane == lax.broadcasted_iota(jnp.int32, (N_HEADS, D_ATTN), 0)
    qblk = jnp.where(own, q_ref[...], 0.0)
    s = jnp.dot(qblk.astype(BF16), kt_ref[...].astype(BF16), preferred_element_type=F32)
    s_self = jnp.sum(qblk * kn, axis=1, keepdims=True)

    dist = l_cache - lax.broadcasted_iota(jnp.int32, (N_HEADS, l_cache), 1)
    cnt = jnp.zeros((N_HEADS, l_cache), F32)
    for w, d in zip(WINDOWS, DILATIONS):
        cnt = cnt + jnp.where((dist <= w) & ((dist & (d - 1)) == 0), 1.0, 0.0)
    s = jnp.where(cnt > 0, s, NEG)
    m = jnp.maximum(jnp.max(s, axis=1, keepdims=True), s_self)
    e = (cnt * jnp.exp(s - m)).astype(BF16)
    e_self = len(WINDOWS) * jnp.exp(s_self - m)
    den = jnp.sum(e.astype(F32), axis=1, keepdims=True) + e_self
    pv = lax.dot_general(e, vt_ref[...].astype(BF16), (((1,), (1,)), ((), ())),
                         preferred_element_type=F32)
    per_lane = lambda a: jnp.sum(jnp.where(own, a, 0.0), axis=0, keepdims=True)
    num = per_lane(pv) + per_lane(jnp.broadcast_to(e_self, own.shape)) * vn
    attn_ref[...] = num / per_lane(jnp.broadcast_to(den, own.shape))


def _sample_attn(q, kf, vf, cache_kt, cache_vt, layer):
    _, bd, _, l_cache = cache_kt.shape
    assert l_cache >= WINDOWS[-1]
    row = pl.BlockSpec((None, 1, D_ATTN), lambda b: (b, 0, 0))
    cache = pl.BlockSpec((None, None, D_ATTN, l_cache), lambda b: (layer, b, 0, 0))
    r3 = lambda a: a.reshape(bd, 1, D_ATTN)
    attn = pl.pallas_call(
        _sample_attn_kernel,
        grid=(bd,),
        in_specs=[row, row, row, cache, cache],
        out_specs=row,
        out_shape=jax.ShapeDtypeStruct((bd, 1, D_ATTN), F32),
        compiler_params=pltpu.CompilerParams(dimension_semantics=("arbitrary",),
                                             vmem_limit_bytes=VMEM_LIMIT),
        name="sample_attn",
    )(r3(q), r3(kf), r3(vf), cache_kt, cache_vt)
    return attn.reshape(bd, D_ATTN)


def _cache_shift_kernel(ck_ref, cv_ref, knt_ref, vnt_ref, nk_ref, nv_ref):
    b = pl.program_id(1)
    l_cache = ck_ref.shape[1]
    bd = knt_ref.shape[1]
    mine = lax.broadcasted_iota(jnp.int32, (D_ATTN, bd), 1) == b
    last_lane = lax.broadcasted_iota(jnp.int32, (D_ATTN, LANES), 1) == LANES - 1
    for c_ref, nt_ref, n_ref in ((ck_ref, knt_ref, nk_ref), (cv_ref, vnt_ref, nv_ref)):
        new_col = jnp.sum(jnp.where(mine, nt_ref[...], 0.0), axis=1, keepdims=True)
        rolled = pltpu.roll(c_ref[...], l_cache - 1, 1)
        n_ref[...] = rolled
        n_ref[:, l_cache - LANES:] = jnp.where(last_lane, new_col, rolled[:, l_cache - LANES:])


def _cache_shift(cache_kt, cache_vt, k_new_t, v_new_t):
    depth, bd, _, l_cache = cache_kt.shape
    cache = pl.BlockSpec((None, None, D_ATTN, l_cache), lambda l, b: (l, b, 0, 0))
    cols = pl.BlockSpec((None, D_ATTN, bd), lambda l, b: (l, 0, 0))
    return pl.pallas_call(
        _cache_shift_kernel,
        grid=(depth, bd),
        in_specs=[cache, cache, cols, cols],
        out_specs=(cache, cache),
        out_shape=(jax.ShapeDtypeStruct(cache_kt.shape, F32),) * 2,
        compiler_params=pltpu.CompilerParams(dimension_semantics=("arbitrary", "arbitrary"),
                                             vmem_limit_bytes=VMEM_LIMIT),
        name="cache_shift",
    )(cache_kt, cache_vt, k_new_t, v_new_t)


def _sample_out_kernel(x_ref, attn_ref, zs_ref, u_ref, gza_ref, bz_ref, uc_ref, sa_ref, sc_ref,
                       aw_ref, ab_ref, ag_ref, alb_ref, cw_ref, w_ref, y_ref, na_ref, nc_ref):
    u = u_ref[...]
    uc = uc_ref[...]
    na = CONV_WIDTH - 1
    nc = SCONV_WIDTH - 1
    ca = ab_ref[...] + aw_ref[na:na + 1, :] * u
    for j in range(na):
        ca = ca + aw_ref[j:j + 1, :] * sa_ref[:, j * D_CONV:(j + 1) * D_CONV]
    ya = _silu(_layernorm_rows(ca, ag_ref[...], alb_ref[...])) * gza_ref[...]
    cc = cw_ref[nc:nc + 1, :] * uc
    for j in range(nc):
        cc = cc + cw_ref[j:j + 1, :] * sc_ref[:, j * D_SCONV:(j + 1) * D_SCONV]
    yc = bz_ref[...] * cc
    acc = jnp.dot((attn_ref[...] * zs_ref[...]).astype(BF16), w_ref[0:D_ATTN, :],
                  preferred_element_type=F32)
    acc = acc + jnp.dot(ya.astype(BF16), w_ref[D_ATTN:D_ATTN + D_CONV, :],
                        preferred_element_type=F32)
    acc = acc + jnp.dot(yc.astype(BF16), w_ref[D_ATTN + D_CONV:, :],
                        preferred_element_type=F32)
    y_ref[...] = x_ref[...] + acc
    na_ref[:, 0:(na - 1) * D_CONV] = sa_ref[:, D_CONV:na * D_CONV]
    na_ref[:, (na - 1) * D_CONV:] = u
    nc_ref[:, 0:(nc - 1) * D_SCONV] = sc_ref[:, D_SCONV:nc * D_SCONV]
    nc_ref[:, (nc - 1) * D_SCONV:] = uc


def _sample_out(x, attn, zs, u, gza, bz, uc, sa, sc, aw, ab, ag, alb, cw, w_bf):
    bd = x.shape[0]
    sa2 = sa.reshape(bd, -1)
    sc2 = sc.reshape(bd, -1)
    args = (x, attn, zs, u, gza, bz, uc, sa2, sc2, aw, ab, ag, alb, cw, w_bf)
    full = lambda a: pl.BlockSpec(a.shape, lambda i: (0,) * a.ndim)
    y, na, nc = pl.pallas_call(
        _sample_out_kernel,
        grid=(1,),
        in_specs=[full(a) for a in args],
        out_specs=(full(x), full(sa2), full(sc2)),
        out_shape=(jax.ShapeDtypeStruct(x.shape, F32),
                   jax.ShapeDtypeStruct(sa2.shape, F32),
                   jax.ShapeDtypeStruct(sc2.shape, F32)),
        compiler_params=pltpu.CompilerParams(dimension_semantics=("arbitrary",),
                                             vmem_limit_bytes=VMEM_LIMIT),
        name="sample_out",
    )(*args)
    return y, na.reshape(sa.shape), nc.reshape(sc.shape)


def _rope_tables(pos):
    half = HEAD_DIM // 2
    inv = ROPE_THETA ** (-jnp.arange(half, dtype=F32) / half)
    ang = pos.astype(F32)[:, None] * inv[None, :]
    cos, sin = jnp.cos(ang), jnp.sin(ang)
    reps = LANES // HEAD_DIM
    cos_t = jnp.tile(jnp.concatenate([cos, cos], axis=-1), (1, reps))
    sin_t = jnp.tile(jnp.concatenate([-sin, sin], axis=-1), (1, reps))
    return cos_t, sin_t


def _segment_mean_matrix():
    seg = np.kron(np.eye(LANES // HEAD_DIM), np.ones((HEAD_DIM, HEAD_DIM))) / HEAD_DIM
    return jnp.asarray(seg, dtype=BF16)


def _to_channel_major(cache):
    lead = cache.shape[:-3]
    n = len(lead)
    perm = tuple(range(n)) + (n + 1, n + 2, n)
    return jnp.transpose(cache, perm).reshape(*lead, D_ATTN, cache.shape[-3])


def _from_channel_major(cache_t):
    lead = cache_t.shape[:-2]
    n = len(lead)
    perm = tuple(range(n)) + (n + 2, n, n + 1)
    return jnp.transpose(cache_t.reshape(*lead, N_HEADS, HEAD_DIM, cache_t.shape[-1]), perm)


PROMPT_TM_IN = 512
PROMPT_TM_OUT = 512


def kernel(x_prompt, x_sample, cache_k, cache_v, state_conv_a, state_conv_c, ln_g, w_in,
           q_norm_g, k_norm_g, a_conv_w, a_conv_b, a_ln_g, a_ln_b, c_conv_w, w_out):
    bp, sp, _ = x_prompt.shape
    bd, ts, _ = x_sample.shape
    assert ts == 1
    depth = w_in.shape[0]
    l_prompt = min(WINDOWS[-1], sp)

    cos_p, sin_p = _rope_tables(jnp.arange(sp, dtype=jnp.int32))
    cos_s, sin_s = _rope_tables(jnp.full((bd,), PAST_LEN, dtype=jnp.int32))
    seg = _segment_mean_matrix()
    row = lambda a: a.reshape(1, -1)
    tile_g = lambda g: jnp.tile(g, LANES // HEAD_DIM).reshape(1, LANES)
    cache_kt = _to_channel_major(cache_k)
    cache_vt = _to_channel_major(cache_v)

    hp = x_prompt
    hs = x_sample.reshape(bd, D_MODEL)
    pk, pv, pa, pc, sk, sv, sa, sc = [], [], [], [], [], [], [], []
    for l in range(depth):
        w_in_bf = w_in[l].astype(BF16)
        w_out_bf = w_out[l].astype(BF16)
        lng, qg, kg = row(ln_g[l]), tile_g(q_norm_g[l]), tile_g(k_norm_g[l])
        conv_w = (a_conv_w[l], row(a_conv_b[l]), row(a_ln_g[l]), row(a_ln_b[l]), c_conv_w[l])

        outs = _in_proj_prompt(hp, lng, w_in_bf, cos_p, sin_p, qg, kg, seg, PROMPT_TM_IN, l_prompt)
        n_pat = len(DILATIONS)
        q_d, k_d, v_d = outs[0:n_pat], outs[n_pat:2 * n_pat], outs[2 * n_pat:3 * n_pat]
        kt, vt, zs, u, gza, bz, uc = outs[3 * n_pat:]
        attn = _attention(list(zip(q_d, k_d, v_d)), bp, sp)
        hp = _out_proj(hp, attn, zs, u, gza, bz, uc, *conv_w, w_out_bf, PROMPT_TM_OUT)
        pk.append(kt)
        pv.append(vt)
        pa.append(u[:, sp - (CONV_WIDTH - 1):])
        pc.append(uc[:, sp - (SCONV_WIDTH - 1):])

        q, kf, vf, zs, u, gza, bz, uc = _in_proj_sample(hs, lng, w_in_bf, cos_s, sin_s, qg, kg, seg)
        attn = _sample_attn(q, kf, vf, cache_kt, cache_vt, l)
        hs, na, nc = _sample_out(hs, attn, zs, u, gza, bz, uc, state_conv_a[l], state_conv_c[l],
                                 *conv_w, w_out_bf)
        sk.append(kf)
        sv.append(vf)
        sa.append(na)
        sc.append(nc)

    new_cols = lambda rows: jnp.transpose(jnp.stack(rows), (0, 2, 1))
    nkt, nvt = _cache_shift(cache_kt, cache_vt, new_cols(sk), new_cols(sv))
    return (hp, hs.reshape(bd, ts, D_MODEL),
            _from_channel_major(jnp.stack(pk)), _from_channel_major(jnp.stack(pv)),
            jnp.stack(pa), jnp.stack(pc),
            _from_channel_major(nkt), _from_channel_major(nvt),
            jnp.stack(sa), jnp.stack(sc))
```

```python
import functools

import numpy as np
import jax
import jax.numpy as jnp
from jax import lax
from jax.experimental import pallas as pl
from jax.experimental.pallas import tpu as pltpu

D_MODEL = 1024
N_HEADS = 8
HEAD_DIM = 64
D_ATTN = N_HEADS * HEAD_DIM
D_CONV = 256
CONV_WIDTH = 31
D_SCONV = 256
SCONV_WIDTH = 3
WINDOWS = (128, 512, 2048)
DILATIONS = (1, 4, 16)
WIN_KEYS = 128
assert all(w // d == WIN_KEYS for w, d in zip(WINDOWS, DILATIONS))
assert all(d & (d - 1) == 0 for d in DILATIONS)
PAST_LEN = 16384
ROPE_THETA = 10000.0
EPS = 1e-6
NEG = -1e30
LOG2_E = 1.4426950408889634

C_Q, C_K, C_V, C_Z = 0, 512, 1024, 1536
C_AVAL, C_AGATE, C_ZA = 2048, 2304, 2560
C_CB, C_CC, C_CH, C_ZC = 2816, 3072, 3328, 3584
D_IN = 3840

LANES = 128
SUBLANES = 8
N_SLABS = D_ATTN // LANES
VMEM_LIMIT = 56 * 1024 * 1024

F32 = jnp.float32
BF16 = jnp.bfloat16


def _sigmoid(x):
    return 1.0 / (1.0 + jnp.exp(-x))


def _silu(x):
    return x * _sigmoid(x)


def _in_proj_body(x_ref, lng_ref, w_ref, cos_ref, sin_ref, qg_ref, kg_ref, seg_ref,
                  zs_ref, u_ref, gza_ref, bz_ref, uc_ref, emit_q, emit_k, emit_v, q_scale):
    x = x_ref[...]
    ms = jnp.mean(x * x, axis=-1, keepdims=True)
    xn = (x * lax.rsqrt(ms + EPS) * lng_ref[...]).astype(BF16)

    def proj(c0, c1):
        return jnp.dot(xn, w_ref[:, c0:c1], preferred_element_type=F32)

    cos = cos_ref[...]
    sin = sin_ref[...]
    tm = x.shape[0]
    first_half = (lax.broadcasted_iota(jnp.int32, (tm, LANES), 1) & (HEAD_DIM // 2)) == 0

    def norm_rope(p, g_ref, scale, emit):
        for s in range(N_SLABS):
            ps = p[:, s * LANES:(s + 1) * LANES]
            ss = jnp.dot((ps * ps).astype(BF16), seg_ref[...], preferred_element_type=F32)
            pn = ps * lax.rsqrt(ss + EPS) * g_ref[...]
            partner = jnp.where(first_half,
                                pltpu.roll(pn, LANES - HEAD_DIM // 2, 1),
                                pltpu.roll(pn, HEAD_DIM // 2, 1))
            emit(s, (pn * cos + partner * sin) * scale)

    norm_rope(proj(C_Q, C_K), qg_ref, q_scale, emit_q)
    norm_rope(proj(C_K, C_V), kg_ref, 1.0, emit_k)
    v = proj(C_V, C_Z)
    for s in range(N_SLABS):
        emit_v(s, v[:, s * LANES:(s + 1) * LANES])
    zs_ref[...] = _silu(proj(C_Z, C_AVAL))
    ag = proj(C_AVAL, C_ZA)
    u_ref[...] = ag[:, :D_CONV] * _sigmoid(ag[:, D_CONV:])
    zb = proj(C_ZA, C_CC)
    gza_ref[...] = _silu(zb[:, :D_CONV])
    zc = proj(C_ZC, D_IN)
    bz_ref[...] = zb[:, D_CONV:] * _silu(zc)
    ch = proj(C_CC, C_ZC)
    uc_ref[...] = ch[:, :D_SCONV] * ch[:, D_SCONV:]


def _in_proj_prompt_kernel(x_ref, lng_ref, w_ref, cos_ref, sin_ref, qg_ref, kg_ref, seg_ref,
                           q1_ref, q4_ref, q16_ref, k1_ref, k4_ref, k16_ref,
                           v1_ref, v4_ref, v16_ref, kt_ref, vt_ref,
                           zs_ref, u_ref, gza_ref, bz_ref, uc_ref, stage_ref, *, tm):
    def emitter(which, outs, t_ref):
        def emit(s, val):
            stage_ref[which, s] = val
            for d, out in zip(DILATIONS, outs):
                if d == 1:
                    out[s, 0] = val.astype(BF16)
                else:
                    for r in range(d):
                        out[s, r] = stage_ref[which, s, pl.ds(r, tm // d, stride=d), :].astype(BF16)
            if t_ref is not None:
                t_ref[s * LANES:(s + 1) * LANES, :] = val.T
        return emit

    _in_proj_body(x_ref, lng_ref, w_ref, cos_ref, sin_ref, qg_ref, kg_ref, seg_ref,
                  zs_ref, u_ref, gza_ref, bz_ref, uc_ref,
                  emitter(0, (q1_ref, q4_ref, q16_ref), None),
                  emitter(1, (k1_ref, k4_ref, k16_ref), kt_ref),
                  emitter(2, (v1_ref, v4_ref, v16_ref), vt_ref),
                  q_scale=HEAD_DIM ** -0.5 * LOG2_E)


def _in_proj_prompt(x, lng, w_bf, cos_t, sin_t, qg, kg, seg, tm, l_tail):
    b, s, _ = x.shape
    n_skip = (s - l_tail) // tm
    tile = lambda c: pl.BlockSpec((None, tm, c), lambda bi, i: (bi, i, 0))
    full = lambda a: pl.BlockSpec(a.shape, lambda bi, i: (0,) * a.ndim)
    pos = pl.BlockSpec((tm, LANES), lambda bi, i: (i, 0))
    deint = lambda d: pl.BlockSpec((None, N_SLABS, d, tm // d, LANES),
                                   lambda bi, i: (bi, 0, 0, i, 0))
    tail = pl.BlockSpec((None, D_ATTN, tm), lambda bi, i: (bi, 0, jnp.maximum(i - n_skip, 0)))
    deint_shape = lambda d: jax.ShapeDtypeStruct((b, N_SLABS, d, s // d, LANES), BF16)
    out_shapes = tuple(deint_shape(d) for _ in range(3) for d in DILATIONS) + (
        jax.ShapeDtypeStruct((b, D_ATTN, l_tail), F32),
        jax.ShapeDtypeStruct((b, D_ATTN, l_tail), F32),
        jax.ShapeDtypeStruct((b, s, D_ATTN), F32),
        jax.ShapeDtypeStruct((b, s, D_CONV), F32),
        jax.ShapeDtypeStruct((b, s, D_CONV), F32),
        jax.ShapeDtypeStruct((b, s, D_SCONV), F32),
        jax.ShapeDtypeStruct((b, s, D_SCONV), F32),
    )
    out_specs = tuple(deint(d) for _ in range(3) for d in DILATIONS) + (
        tail, tail, tile(D_ATTN), tile(D_CONV), tile(D_CONV), tile(D_SCONV), tile(D_SCONV))
    return pl.pallas_call(
        functools.partial(_in_proj_prompt_kernel, tm=tm),
        grid=(b, s // tm),
        in_specs=[tile(D_MODEL), full(lng), full(w_bf), pos, pos, full(qg), full(kg), full(seg)],
        out_specs=out_specs,
        out_shape=out_shapes,
        scratch_shapes=[pltpu.VMEM((3, N_SLABS, tm, LANES), F32)],
        compiler_params=pltpu.CompilerParams(dimension_semantics=("arbitrary", "arbitrary"),
                                             vmem_limit_bytes=VMEM_LIMIT),
        name="in_proj",
    )(x, lng, w_bf, cos_t, sin_t, qg, kg, seg)


def _in_proj_sample_kernel(x_ref, lng_ref, w_ref, cos_ref, sin_ref, qg_ref, kg_ref, seg_ref,
                           q_ref, k_ref, v_ref, zs_ref, u_ref, gza_ref, bz_ref, uc_ref):
    def emitter(out):
        def emit(s, val):
            out[:, s * LANES:(s + 1) * LANES] = val
        return emit

    _in_proj_body(x_ref, lng_ref, w_ref, cos_ref, sin_ref, qg_ref, kg_ref, seg_ref,
                  zs_ref, u_ref, gza_ref, bz_ref, uc_ref,
                  emitter(q_ref), emitter(k_ref), emitter(v_ref), q_scale=HEAD_DIM ** -0.5)


def _in_proj_sample(x2d, lng, w_bf, cos_t, sin_t, qg, kg, seg):
    n = x2d.shape[0]
    args = (x2d, lng, w_bf, cos_t, sin_t, qg, kg, seg)
    full = lambda a: pl.BlockSpec(a.shape, lambda i: (0,) * a.ndim)
    widths = (D_ATTN, D_ATTN, D_ATTN, D_ATTN, D_CONV, D_CONV, D_SCONV, D_SCONV)
    out_shapes = tuple(jax.ShapeDtypeStruct((n, c), F32) for c in widths)
    return pl.pallas_call(
        _in_proj_sample_kernel,
        grid=(1,),
        in_specs=[full(a) for a in args],
        out_specs=tuple(full(s) for s in out_shapes),
        out_shape=out_shapes,
        compiler_params=pltpu.CompilerParams(dimension_semantics=("arbitrary",),
                                             vmem_limit_bytes=VMEM_LIMIT),
        name="in_proj_sample",
    )(*args)


ATTN_TILE = 2048
assert all(ATTN_TILE % (WIN_KEYS * d) == 0 for d in DILATIONS)
ATTN_UNROLL = 8
assert (ATTN_TILE // WIN_KEYS) % ATTN_UNROLL == 0


def _attn_kernel(*refs, tile):
    n_pat = len(DILATIONS)
    qkv_refs = [refs[3 * p:3 * p + 3] for p in range(n_pat)]
    bias_ref, o_ref, acc_ref, m_ref, l_ref = refs[3 * n_pat:]
    t = pl.program_id(2)
    blk = WIN_KEYS
    n_blocks = tile // blk
    lane = lax.broadcasted_iota(jnp.int32, (1, LANES), 1)
    head_a = lane < HEAD_DIM

    def block_softmax(qb, k_ref, v_ref, r, q0):
        kk = pl.multiple_of(jnp.maximum(q0 - blk, 0), blk)
        bias = bias_ref[jnp.minimum(q0, 1)]
        kb = k_ref[r, pl.ds(kk, 2 * blk), :]
        vb = v_ref[r, pl.ds(kk, 2 * blk), :]
        res, mx = [], []
        for sel in (head_a, jnp.logical_not(head_a)):
            qh = jnp.where(sel, qb, jnp.zeros_like(qb))
            s = lax.dot_general(qh, kb, (((1,), (1,)), ((), ())), preferred_element_type=F32)
            s = s + bias
            m = jnp.max(s, axis=1, keepdims=True)
            p = jnp.exp2(s - m).astype(BF16)
            vh = jnp.where(sel, vb, jnp.ones_like(vb))
            res.append(jnp.dot(p, vh, preferred_element_type=F32))
            mx.append(m)
        num = jnp.where(head_a, res[0], res[1])
        den = pltpu.roll(jnp.where(head_a, res[1], res[0]), HEAD_DIM, 1)
        return num, den, jnp.where(head_a, mx[0], mx[1])

    order = sorted(range(n_pat), key=lambda p: -DILATIONS[p])
    for step, p in enumerate(order):
        d = DILATIONS[p]
        q_ref, k_ref, v_ref = qkv_refs[p]
        per_res = n_blocks // d
        assert per_res & (per_res - 1) == 0
        first, last = step == 0, step == n_pat - 1

        def body(it, carry, d=d, q_ref=q_ref, k_ref=k_ref, v_ref=v_ref, per_res=per_res,
                 first=first, last=last):
            for u in range(ATTN_UNROLL):
                i = it * ATTN_UNROLL + u
                r = lax.shift_right_logical(i, per_res.bit_length() - 1)
                jb = i & (per_res - 1)
                qb = q_ref[r, pl.ds(pl.multiple_of(jb * blk, blk), blk), :]
                num, den, mb = block_softmax(qb, k_ref, v_ref, r, t * (tile // d) + jb * blk)
                start = jb * blk * d + r
                rows = pl.ds(start, blk) if d == 1 else pl.ds(start, blk, stride=d)
                if not first:
                    m_old = m_ref[rows, :]
                    m_new = jnp.maximum(m_old, mb)
                    w_old = jnp.exp2(m_old - m_new)
                    w_blk = jnp.exp2(mb - m_new)
                    num = acc_ref[rows, :] * w_old + num * w_blk
                    den = l_ref[rows, :] * w_old + den * w_blk
                    mb = m_new
                if last:
                    o_ref[rows, :] = num / den
                else:
                    acc_ref[rows, :] = num
                    l_ref[rows, :] = den
                    m_ref[rows, :] = mb
            return carry

        lax.fori_loop(0, n_blocks // ATTN_UNROLL, body, 0)


def _band_bias():
    rel = np.arange(2 * WIN_KEYS)[None, :] - np.arange(WIN_KEYS)[:, None]
    variants = [rel - off for off in (0, WIN_KEYS)]
    return jnp.asarray(np.stack([np.where((v <= 0) & (v >= -WIN_KEYS), 0.0, NEG)
                                 for v in variants]), dtype=F32)


def _attention(qkv, b, s):
    tile = ATTN_TILE
    in_specs, args = [], []
    for d, (q, k, v) in zip(DILATIONS, qkv):
        qspec = pl.BlockSpec((None, None, d, tile // d, LANES), lambda bi, c, i: (bi, c, 0, i, 0))
        kvspec = pl.BlockSpec((None, None, d, s // d, LANES), lambda bi, c, i: (bi, c, 0, 0, 0))
        in_specs += [qspec, kvspec, kvspec]
        args += [q, k, v]
    bias = _band_bias()
    in_specs.append(pl.BlockSpec(bias.shape, lambda bi, c, i: (0, 0, 0)))
    args.append(bias)
    return pl.pallas_call(
        functools.partial(_attn_kernel, tile=tile),
        grid=(b, N_SLABS, s // tile),
        in_specs=in_specs,
        out_specs=pl.BlockSpec((None, tile, LANES), lambda bi, c, i: (bi, i, c)),
        out_shape=jax.ShapeDtypeStruct((b, s, D_ATTN), F32),
        scratch_shapes=[pltpu.VMEM((tile, LANES), F32)] * 3,
        compiler_params=pltpu.CompilerParams(
            dimension_semantics=("arbitrary", "arbitrary", "arbitrary"),
            vmem_limit_bytes=VMEM_LIMIT),
        name="dilated_attn",
    )(*args)


A_HALO = 32
C_HALO = 8


def _layernorm_rows(x, g, b):
    mu = jnp.mean(x, axis=-1, keepdims=True)
    xc = x - mu
    var = jnp.mean(xc * xc, axis=-1, keepdims=True)
    return xc * lax.rsqrt(var + EPS) * g + b


def _out_proj_kernel(x_ref, attn_ref, zs_ref, u_ref, uh_ref, gza_ref, bz_ref, uc_ref, uch_ref,
                     aw_ref, ab_ref, ag_ref, alb_ref, cw_ref, w_ref,
                     y_ref, ubuf, uphase, ucbuf, *, tm):
    first_tile = pl.program_id(1) == 0
    acc = jnp.dot((attn_ref[...] * zs_ref[...]).astype(BF16), w_ref[0:D_ATTN, :],
                  preferred_element_type=F32)

    ubuf[0:A_HALO, :] = jnp.where(first_tile, 0.0, uh_ref[...])
    ubuf[A_HALO:A_HALO + tm, :] = u_ref[...]
    n_rows = tm + A_HALO - SUBLANES
    for b in range(1, SUBLANES):
        uphase[b - 1] = ubuf[b:b + n_rows, :]
    base = A_HALO - (CONV_WIDTH - 1)
    groups = (tm // SUBLANES, SUBLANES, D_CONV)
    ca = jnp.zeros(groups, F32)
    for j in range(CONV_WIDTH):
        b = (base + j) % SUBLANES
        a8 = base + j - b
        rows = ubuf[a8:a8 + tm, :] if b == 0 else uphase[b - 1, a8:a8 + tm, :]
        ca = ca + aw_ref[j][None] * rows.reshape(groups)
    ca = ca.reshape(tm, D_CONV) + ab_ref[...]
    ya = _silu(_layernorm_rows(ca, ag_ref[...], alb_ref[...])) * gza_ref[...]
    acc = acc + jnp.dot(ya.astype(BF16), w_ref[D_ATTN:D_ATTN + D_CONV, :],
                        preferred_element_type=F32)

    ucbuf[0:C_HALO, :] = jnp.where(first_tile, 0.0, uch_ref[...])
    ucbuf[C_HALO:C_HALO + tm, :] = uc_ref[...]
    cbase = C_HALO - (SCONV_WIDTH - 1)
    cc = jnp.zeros((tm, D_SCONV), F32)
    for j in range(SCONV_WIDTH):
        cc = cc + cw_ref[j:j + 1, :] * ucbuf[cbase + j:cbase + j + tm, :]
    yc = bz_ref[...] * cc
    acc = acc + jnp.dot(yc.astype(BF16), w_ref[D_ATTN + D_CONV:, :],
                        preferred_element_type=F32)

    y_ref[...] = x_ref[...] + acc


def _out_proj(x, attn, zs, u, gza, bz, uc, aw, ab, ag, alb, cw, w_bf, tm):
    b, s, _ = x.shape
    tile = lambda c: pl.BlockSpec((None, tm, c), lambda bi, i: (bi, i, 0))
    halo = lambda h, c: pl.BlockSpec(
        (None, h, c), lambda bi, i: (bi, jnp.maximum(i * (tm // h) - 1, 0), 0))
    full = lambda a: pl.BlockSpec(a.shape, lambda bi, i: (0,) * a.ndim)
    return pl.pallas_call(
        functools.partial(_out_proj_kernel, tm=tm),
        grid=(b, s // tm),
        in_specs=[tile(D_MODEL), tile(D_ATTN), tile(D_ATTN),
                  tile(D_CONV), halo(A_HALO, D_CONV), tile(D_CONV), tile(D_SCONV),
                  tile(D_SCONV), halo(C_HALO, D_SCONV)]
                 + [full(aw), full(ab), full(ag), full(alb), full(cw), full(w_bf)],
        out_specs=tile(D_MODEL),
        out_shape=jax.ShapeDtypeStruct((b, s, D_MODEL), F32),
        scratch_shapes=[pltpu.VMEM((A_HALO + tm, D_CONV), F32),
                        pltpu.VMEM((SUBLANES - 1, A_HALO + tm - SUBLANES, D_CONV), F32),
                        pltpu.VMEM((C_HALO + tm, D_SCONV), F32)],
        compiler_params=pltpu.CompilerParams(dimension_semantics=("arbitrary", "arbitrary"),
                                             vmem_limit_bytes=VMEM_LIMIT),
        name="out_proj",
    )(x, attn, zs, u, u, gza, bz, uc, uc, aw, ab, ag, alb, cw, w_bf)


def _sample_attn_kernel(q_ref, kn_ref, vn_ref, kt_ref, vt_ref, attn_ref):
    l_cache = kt_ref.shape[1]
    kn = kn_ref[...]
    vn = vn_ref[...]
    head_of_lane = lax.broadcasted_iota(jnp.int32, (N_HEADS, D_ATTN), 1) // HEAD_DIM
    own = head_of_lane == lax.broadcasted_iota(jnp.int32, (N_HEADS, D_ATTN), 0)
    qblk = jnp.where(own, q_ref[...], 0.0)
    s = jnp.dot(qblk.astype(BF16), kt_ref[...].astype(BF16), preferred_element_type=F32)
    s_self = jnp.sum(qblk * kn, axis=1, keepdims=True)

    dist = l_cache - lax.broadcasted_iota(jnp.int32, (N_HEADS, l_cache), 1)
    cnt = jnp.zeros((N_HEADS, l_cache), F32)
    for w, d in zip(WINDOWS, DILATIONS):
        cnt = cnt + jnp.where((dist <= w) & ((dist & (d - 1)) == 0), 1.0, 0.0)
    s = jnp.where(cnt > 0, s, NEG)
    m = jnp.maximum(jnp.max(s, axis=1, keepdims=True), s_self)
    e = (cnt * jnp.exp(s - m)).astype(BF16)
    e_self = len(WINDOWS) * jnp.exp(s_self - m)
    den = jnp.sum(e.astype(F32), axis=1, keepdims=True) + e_self
    pv = lax.dot_general(e, vt_ref[...].astype(BF16), (((1,), (1,)), ((), ())),
                         preferred_element_type=F32)
    per_lane = lambda a: jnp.sum(jnp.where(own, a, 0.0), axis=0, keepdims=True)
    num = per_lane(pv) + per_lane(jnp.broadcast_to(e_self, own.shape)) * vn
    attn_ref[...] = num / per_lane(jnp.broadcast_to(den, own.shape))


def _sample_attn(q, kf, vf, cache_kt, cache_vt, layer):
    _, bd, _, l_cache = cache_kt.shape
    assert l_cache >= WINDOWS[-1]
    row = pl.BlockSpec((None, 1, D_ATTN), lambda b: (b, 0, 0))
    cache = pl.BlockSpec((None, None, D_ATTN, l_cache), lambda b: (layer, b, 0, 0))
    r3 = lambda a: a.reshape(bd, 1, D_ATTN)
    attn = pl.pallas_call(
        _sample_attn_kernel,
        grid=(bd,),
        in_specs=[row, row, row, cache, cache],
        out_specs=row,
        out_shape=jax.ShapeDtypeStruct((bd, 1, D_ATTN), F32),
        compiler_params=pltpu.CompilerParams(dimension_semantics=("arbitrary",),
                                             vmem_limit_bytes=VMEM_LIMIT),
        name="sample_attn",
    )(r3(q), r3(kf), r3(vf), cache_kt, cache_vt)
    return attn.reshape(bd, D_ATTN)


def _cache_shift_kernel(ck_ref, cv_ref, knt_ref, vnt_ref, nk_ref, nv_ref):
    b = pl.program_id(1)
    l_cache = ck_ref.shape[1]
    bd = knt_ref.shape[1]
    mine = lax.broadcasted_iota(jnp.int32, (D_ATTN, bd), 1) == b
    last_lane = lax.broadcasted_iota(jnp.int32, (D_ATTN, LANES), 1) == LANES - 1
    for c_ref, nt_ref, n_ref in ((ck_ref, knt_ref, nk_ref), (cv_ref, vnt_ref, nv_ref)):
        new_col = jnp.sum(jnp.where(mine, nt_ref[...], 0.0), axis=1, keepdims=True)
        rolled = pltpu.roll(c_ref[...], l_cache - 1, 1)
        n_ref[...] = rolled
        n_ref[:, l_cache - LANES:] = jnp.where(last_lane, new_col, rolled[:, l_cache - LANES:])


def _cache_shift(cache_kt, cache_vt, k_new_t, v_new_t):
    depth, bd, _, l_cache = cache_kt.shape
    cache = pl.BlockSpec((None, None, D_ATTN, l_cache), lambda l, b: (l, b, 0, 0))
    cols = pl.BlockSpec((None, D_ATTN, bd), lambda l, b: (l, 0, 0))
    return pl.pallas_call(
        _cache_shift_kernel,
        grid=(depth, bd),
        in_specs=[cache, cache, cols, cols],
        out_specs=(cache, cache),
        out_shape=(jax.ShapeDtypeStruct(cache_kt.shape, F32),) * 2,
        compiler_params=pltpu.CompilerParams(dimension_semantics=("arbitrary", "arbitrary"),
                                             vmem_limit_bytes=VMEM_LIMIT),
        name="cache_shift",
    )(cache_kt, cache_vt, k_new_t, v_new_t)


def _sample_out_kernel(x_ref, attn_ref, zs_ref, u_ref, gza_ref, bz_ref, uc_ref, sa_ref, sc_ref,
                       aw_ref, ab_ref, ag_ref, alb_ref, cw_ref, w_ref, y_ref, na_ref, nc_ref):
    u = u_ref[...]
    uc = uc_ref[...]
    na = CONV_WIDTH - 1
    nc = SCONV_WIDTH - 1
    ca = ab_ref[...] + aw_ref[na:na + 1, :] * u
    for j in range(na):
        ca = ca + aw_ref[j:j + 1, :] * sa_ref[:, j * D_CONV:(j + 1) * D_CONV]
    ya = _silu(_layernorm_rows(ca, ag_ref[...], alb_ref[...])) * gza_ref[...]
    cc = cw_ref[nc:nc + 1, :] * uc
    for j in range(nc):
        cc = cc + cw_ref[j:j + 1, :] * sc_ref[:, j * D_SCONV:(j + 1) * D_SCONV]
    yc = bz_ref[...] * cc
    acc = jnp.dot((attn_ref[...] * zs_ref[...]).astype(BF16), w_ref[0:D_ATTN, :],
                  preferred_element_type=F32)
    acc = acc + jnp.dot(ya.astype(BF16), w_ref[D_ATTN:D_ATTN + D_CONV, :],
                        preferred_element_type=F32)
    acc = acc + jnp.dot(yc.astype(BF16), w_ref[D_ATTN + D_CONV:, :],
                        preferred_element_type=F32)
    y_ref[...] = x_ref[...] + acc
    na_ref[:, 0:(na - 1) * D_CONV] = sa_ref[:, D_CONV:na * D_CONV]
    na_ref[:, (na - 1) * D_CONV:] = u
    nc_ref[:, 0:(nc - 1) * D_SCONV] = sc_ref[:, D_SCONV:nc * D_SCONV]
    nc_ref[:, (nc - 1) * D_SCONV:] = uc


def _sample_out(x, attn, zs, u, gza, bz, uc, sa, sc, aw, ab, ag, alb, cw, w_bf):
    bd = x.shape[0]
    sa2 = sa.reshape(bd, -1)
    sc2 = sc.reshape(bd, -1)
    args = (x, attn, zs, u, gza, bz, uc, sa2, sc2, aw, ab, ag, alb, cw, w_bf)
    full = lambda a: pl.BlockSpec(a.shape, lambda i: (0,) * a.ndim)
    y, na, nc = pl.pallas_call(
        _sample_out_kernel,
        grid=(1,),
        in_specs=[full(a) for a in args],
        out_specs=(full(x), full(sa2), full(sc2)),
        out_shape=(jax.ShapeDtypeStruct(x.shape, F32),
                   jax.ShapeDtypeStruct(sa2.shape, F32),
                   jax.ShapeDtypeStruct(sc2.shape, F32)),
        compiler_params=pltpu.CompilerParams(dimension_semantics=("arbitrary",),
                                             vmem_limit_bytes=VMEM_LIMIT),
        name="sample_out",
    )(*args)
    return y, na.reshape(sa.shape), nc.reshape(sc.shape)


def _rope_tables(pos):
    half = HEAD_DIM // 2
    inv = ROPE_THETA ** (-jnp.arange(half, dtype=F32) / half)
    ang = pos.astype(F32)[:, None] * inv[None, :]
    cos, sin = jnp.cos(ang), jnp.sin(ang)
    reps = LANES // HEAD_DIM
    cos_t = jnp.tile(jnp.concatenate([cos, cos], axis=-1), (1, reps))
    sin_t = jnp.tile(jnp.concatenate([-sin, sin], axis=-1), (1, reps))
    return cos_t, sin_t


def _segment_mean_matrix():
    seg = np.kron(np.eye(LANES // HEAD_DIM), np.ones((HEAD_DIM, HEAD_DIM))) / HEAD_DIM
    return jnp.asarray(seg, dtype=BF16)


def _to_channel_major(cache):
    lead = cache.shape[:-3]
    n = len(lead)
    perm = tuple(range(n)) + (n + 1, n + 2, n)
    return jnp.transpose(cache, perm).reshape(*lead, D_ATTN, cache.shape[-3])


def _from_channel_major(cache_t):
    lead = cache_t.shape[:-2]
    n = len(lead)
    perm = tuple(range(n)) + (n + 2, n, n + 1)
    return jnp.transpose(cache_t.reshape(*lead, N_HEADS, HEAD_DIM, cache_t.shape[-1]), perm)


PROMPT_TM_IN = 512
PROMPT_TM_OUT = 512


def kernel(x_prompt, x_sample, cache_k, cache_v, state_conv_a, state_conv_c, ln_g, w_in,
           q_norm_g, k_norm_g, a_conv_w, a_conv_b, a_ln_g, a_ln_b, c_conv_w, w_out):
    bp, sp, _ = x_prompt.shape
    bd, ts, _ = x_sample.shape
    assert ts == 1
    depth = w_in.shape[0]
    l_prompt = min(WINDOWS[-1], sp)

    cos_p, sin_p = _rope_tables(jnp.arange(sp, dtype=jnp.int32))
    cos_s, sin_s = _rope_tables(jnp.full((bd,), PAST_LEN, dtype=jnp.int32))
    seg = _segment_mean_matrix()
    row = lambda a: a.reshape(1, -1)
    tile_g = lambda g: jnp.tile(g, LANES // HEAD_DIM).reshape(1, LANES)
    cache_kt = _to_channel_major(cache_k)
    cache_vt = _to_channel_major(cache_v)

    hp = x_prompt
    hs = x_sample.reshape(bd, D_MODEL)
    pk, pv, pa, pc, sk, sv, sa, sc = [], [], [], [], [], [], [], []
    for l in range(depth):
        w_in_bf = w_in[l].astype(BF16)
        w_out_bf = w_out[l].astype(BF16)
        lng, qg, kg = row(ln_g[l]), tile_g(q_norm_g[l]), tile_g(k_norm_g[l])
        conv_w = (a_conv_w[l], row(a_conv_b[l]), row(a_ln_g[l]), row(a_ln_b[l]), c_conv_w[l])

        outs = _in_proj_prompt(hp, lng, w_in_bf, cos_p, sin_p, qg, kg, seg, PROMPT_TM_IN, l_prompt)
        n_pat = len(DILATIONS)
        q_d, k_d, v_d = outs[0:n_pat], outs[n_pat:2 * n_pat], outs[2 * n_pat:3 * n_pat]
        kt, vt, zs, u, gza, bz, uc = outs[3 * n_pat:]
        attn = _attention(list(zip(q_d, k_d, v_d)), bp, sp)
        aw8 = jnp.broadcast_to(a_conv_w[l][:, None, :], (CONV_WIDTH, SUBLANES, D_CONV))
        hp = _out_proj(hp, attn, zs, u, gza, bz, uc, aw8, *conv_w[1:], w_out_bf, PROMPT_TM_OUT)
        pk.append(kt)
        pv.append(vt)
        pa.append(u[:, sp - (CONV_WIDTH - 1):])
        pc.append(uc[:, sp - (SCONV_WIDTH - 1):])

        q, kf, vf, zs, u, gza, bz, uc = _in_proj_sample(hs, lng, w_in_bf, cos_s, sin_s, qg, kg, seg)
        attn = _sample_attn(q, kf, vf, cache_kt, cache_vt, l)
        hs, na, nc = _sample_out(hs, attn, zs, u, gza, bz, uc, state_conv_a[l], state_conv_c[l],
                                 *conv_w, w_out_bf)
        sk.append(kf)
        sv.append(vf)
        sa.append(na)
        sc.append(nc)

    new_cols = lambda rows: jnp.transpose(jnp.stack(rows), (0, 2, 1))
    nkt, nvt = _cache_shift(cache_kt, cache_vt, new_cols(sk), new_cols(sv))
    return (hp, hs.reshape(bd, ts, D_MODEL),
            _from_channel_major(jnp.stack(pk)), _from_channel_major(jnp.stack(pv)),
            jnp.stack(pa), jnp.stack(pc),
            _from_channel_major(nkt), _from_channel_major(nvt),
            jnp.stack(sa), jnp.stack(sc))
```

```python
import functools

import numpy as np
import jax
import jax.numpy as jnp
from jax import lax
from jax.experimental import pallas as pl
from jax.experimental.pallas import tpu as pltpu

D_MODEL = 1024
N_HEADS = 8
HEAD_DIM = 64
D_ATTN = N_HEADS * HEAD_DIM
D_CONV = 256
CONV_WIDTH = 31
D_SCONV = 256
SCONV_WIDTH = 3
WINDOWS = (128, 512, 2048)
DILATIONS = (1, 4, 16)
WIN_KEYS = 128
assert all(w // d == WIN_KEYS for w, d in zip(WINDOWS, DILATIONS))
assert all(d & (d - 1) == 0 for d in DILATIONS)
PAST_LEN = 16384
ROPE_THETA = 10000.0
EPS = 1e-6
NEG = -1e30
LOG2_E = 1.4426950408889634

C_Q, C_K, C_V, C_Z = 0, 512, 1024, 1536
C_AVAL, C_AGATE, C_ZA = 2048, 2304, 2560
C_CB, C_CC, C_CH, C_ZC = 2816, 3072, 3328, 3584
D_IN = 3840

LANES = 128
SUBLANES = 8
N_SLABS = D_ATTN // LANES
VMEM_LIMIT = 56 * 1024 * 1024

F32 = jnp.float32
BF16 = jnp.bfloat16


def _sigmoid(x):
    return 1.0 / (1.0 + jnp.exp(-x))


def _silu(x):
    return x * _sigmoid(x)


SHIFT_ROWS = 32


def _shift_cache_block(c_ref, nt_ref, n_ref, b):
    l_cache = c_ref.shape[1]
    bd = nt_ref.shape[1]
    mine = lax.broadcasted_iota(jnp.int32, (SHIFT_ROWS, bd), 1) == b
    last_lane = lax.broadcasted_iota(jnp.int32, (SHIFT_ROWS, LANES), 1) == LANES - 1
    for c in range(D_ATTN // SHIFT_ROWS):
        rows = slice(c * SHIFT_ROWS, (c + 1) * SHIFT_ROWS)
        new_col = jnp.sum(jnp.where(mine, nt_ref[rows, :], 0.0), axis=1, keepdims=True)
        rolled = pltpu.roll(c_ref[rows, :], l_cache - 1, 1)
        n_ref[rows, 0:l_cache - LANES] = rolled[:, 0:l_cache - LANES]
        n_ref[rows, l_cache - LANES:] = jnp.where(last_lane, new_col, rolled[:, l_cache - LANES:])


def _shift_specs(depth, bd, l_cache, step_of):
    cache = pl.BlockSpec((None, None, D_ATTN, l_cache),
                         lambda *g: (step_of(*g) // bd, step_of(*g) % bd, 0, 0))
    cols = pl.BlockSpec((None, D_ATTN, bd), lambda *g: (step_of(*g) // bd, 0, 0))
    return cache, cols


def _in_proj_body(x_ref, lng_ref, w_ref, cos_ref, sin_ref, qg_ref, kg_ref, seg_ref,
                  zs_ref, u_ref, gza_ref, bz_ref, uc_ref, emit_q, emit_k, emit_v, q_scale):
    x = x_ref[...]
    ms = jnp.mean(x * x, axis=-1, keepdims=True)
    xn = (x * lax.rsqrt(ms + EPS) * lng_ref[...]).astype(BF16)

    def proj(c0, c1):
        return jnp.dot(xn, w_ref[:, c0:c1], preferred_element_type=F32)

    cos = cos_ref[...]
    sin = sin_ref[...]
    tm = x.shape[0]
    first_half = (lax.broadcasted_iota(jnp.int32, (tm, LANES), 1) & (HEAD_DIM // 2)) == 0

    def norm_rope(p, g_ref, scale, emit):
        for s in range(N_SLABS):
            ps = p[:, s * LANES:(s + 1) * LANES]
            ss = jnp.dot((ps * ps).astype(BF16), seg_ref[...], preferred_element_type=F32)
            pn = ps * lax.rsqrt(ss + EPS) * g_ref[...]
            partner = jnp.where(first_half,
                                pltpu.roll(pn, LANES - HEAD_DIM // 2, 1),
                                pltpu.roll(pn, HEAD_DIM // 2, 1))
            emit(s, (pn * cos + partner * sin) * scale)

    norm_rope(proj(C_Q, C_K), qg_ref, q_scale, emit_q)
    norm_rope(proj(C_K, C_V), kg_ref, 1.0, emit_k)
    v = proj(C_V, C_Z)
    for s in range(N_SLABS):
        emit_v(s, v[:, s * LANES:(s + 1) * LANES])
    zs_ref[...] = _silu(proj(C_Z, C_AVAL))
    ag = proj(C_AVAL, C_ZA)
    u_ref[...] = ag[:, :D_CONV] * _sigmoid(ag[:, D_CONV:])
    zb = proj(C_ZA, C_CC)
    gza_ref[...] = _silu(zb[:, :D_CONV])
    zc = proj(C_ZC, D_IN)
    bz_ref[...] = zb[:, D_CONV:] * _silu(zc)
    ch = proj(C_CC, C_ZC)
    uc_ref[...] = ch[:, :D_SCONV] * ch[:, D_SCONV:]


def _in_proj_prompt_kernel(x_ref, lng_ref, w_ref, cos_ref, sin_ref, qg_ref, kg_ref, seg_ref,
                           *rest, tm, shift):
    if shift:
        (c_ref, nt_ref, q1_ref, q4_ref, q16_ref, k1_ref, k4_ref, k16_ref, v1_ref, v4_ref, v16_ref,
         kt_ref, vt_ref, zs_ref, u_ref, gza_ref, bz_ref, uc_ref, n_ref, stage_ref) = rest
        grid_step = pl.program_id(0) * pl.num_programs(1) + pl.program_id(1)
        _shift_cache_block(c_ref, nt_ref, n_ref, grid_step % nt_ref.shape[1])
    else:
        (q1_ref, q4_ref, q16_ref, k1_ref, k4_ref, k16_ref, v1_ref, v4_ref, v16_ref,
         kt_ref, vt_ref, zs_ref, u_ref, gza_ref, bz_ref, uc_ref, stage_ref) = rest

    def emitter(which, outs, t_ref):
        def emit(s, val):
            stage_ref[which, s] = val
            for d, out in zip(DILATIONS, outs):
                if d == 1:
                    out[s, 0] = val.astype(BF16)
                else:
                    for r in range(d):
                        out[s, r] = stage_ref[which, s, pl.ds(r, tm // d, stride=d), :].astype(BF16)
            if t_ref is not None:
                t_ref[s * LANES:(s + 1) * LANES, :] = val.T
        return emit

    _in_proj_body(x_ref, lng_ref, w_ref, cos_ref, sin_ref, qg_ref, kg_ref, seg_ref,
                  zs_ref, u_ref, gza_ref, bz_ref, uc_ref,
                  emitter(0, (q1_ref, q4_ref, q16_ref), None),
                  emitter(1, (k1_ref, k4_ref, k16_ref), kt_ref),
                  emitter(2, (v1_ref, v4_ref, v16_ref), vt_ref),
                  q_scale=HEAD_DIM ** -0.5 * LOG2_E)


def _in_proj_prompt(x, lng, w_bf, cos_t, sin_t, qg, kg, seg, tm, l_tail, shift=None):
    b, s, _ = x.shape
    grid = (b, s // tm)
    n_skip = (s - l_tail) // tm
    tile = lambda c: pl.BlockSpec((None, tm, c), lambda bi, i: (bi, i, 0))
    full = lambda a: pl.BlockSpec(a.shape, lambda bi, i: (0,) * a.ndim)
    pos = pl.BlockSpec((tm, LANES), lambda bi, i: (i, 0))
    deint = lambda d: pl.BlockSpec((None, N_SLABS, d, tm // d, LANES),
                                   lambda bi, i: (bi, 0, 0, i, 0))
    tail = pl.BlockSpec((None, D_ATTN, tm), lambda bi, i: (bi, 0, jnp.maximum(i - n_skip, 0)))
    deint_shape = lambda d: jax.ShapeDtypeStruct((b, N_SLABS, d, s // d, LANES), BF16)
    out_shapes = tuple(deint_shape(d) for _ in range(3) for d in DILATIONS) + (
        jax.ShapeDtypeStruct((b, D_ATTN, l_tail), F32),
        jax.ShapeDtypeStruct((b, D_ATTN, l_tail), F32),
        jax.ShapeDtypeStruct((b, s, D_ATTN), F32),
        jax.ShapeDtypeStruct((b, s, D_CONV), F32),
        jax.ShapeDtypeStruct((b, s, D_CONV), F32),
        jax.ShapeDtypeStruct((b, s, D_SCONV), F32),
        jax.ShapeDtypeStruct((b, s, D_SCONV), F32),
    )
    out_specs = tuple(deint(d) for _ in range(3) for d in DILATIONS) + (
        tail, tail, tile(D_ATTN), tile(D_CONV), tile(D_CONV), tile(D_SCONV), tile(D_SCONV))
    w_spec = pl.BlockSpec(w_bf.shape, lambda bi, i: (0, 0), pipeline_mode=pl.Buffered(1))
    in_specs = [tile(D_MODEL), full(lng), w_spec, pos, pos, full(qg), full(kg), full(seg)]
    args = [x, lng, w_bf, cos_t, sin_t, qg, kg, seg]
    if shift is not None:
        cache, cols = shift
        depth, bd, _, l_cache = cache.shape
        assert depth * bd == grid[0] * grid[1]
        cache_spec, cols_spec = _shift_specs(depth, bd, l_cache, lambda bi, i: bi * grid[1] + i)
        in_specs += [cache_spec, cols_spec]
        args += [cache, cols]
        out_specs += (cache_spec,)
        out_shapes += (jax.ShapeDtypeStruct(cache.shape, F32),)
    return pl.pallas_call(
        functools.partial(_in_proj_prompt_kernel, tm=tm, shift=shift is not None),
        grid=grid,
        in_specs=in_specs,
        out_specs=out_specs,
        out_shape=out_shapes,
        scratch_shapes=[pltpu.VMEM((3, N_SLABS, tm, LANES), F32)],
        compiler_params=pltpu.CompilerParams(dimension_semantics=("arbitrary", "arbitrary"),
                                             vmem_limit_bytes=VMEM_LIMIT),
        name="in_proj",
    )(*args)


def _in_proj_sample_kernel(x_ref, lng_ref, w_ref, cos_ref, sin_ref, qg_ref, kg_ref, seg_ref,
                           q_ref, k_ref, v_ref, zs_ref, u_ref, gza_ref, bz_ref, uc_ref):
    def emitter(out):
        def emit(s, val):
            out[:, s * LANES:(s + 1) * LANES] = val
        return emit

    _in_proj_body(x_ref, lng_ref, w_ref, cos_ref, sin_ref, qg_ref, kg_ref, seg_ref,
                  zs_ref, u_ref, gza_ref, bz_ref, uc_ref,
                  emitter(q_ref), emitter(k_ref), emitter(v_ref), q_scale=HEAD_DIM ** -0.5)


def _in_proj_sample(x2d, lng, w_bf, cos_t, sin_t, qg, kg, seg):
    n = x2d.shape[0]
    args = (x2d, lng, w_bf, cos_t, sin_t, qg, kg, seg)
    full = lambda a: pl.BlockSpec(a.shape, lambda i: (0,) * a.ndim)
    widths = (D_ATTN, D_ATTN, D_ATTN, D_ATTN, D_CONV, D_CONV, D_SCONV, D_SCONV)
    out_shapes = tuple(jax.ShapeDtypeStruct((n, c), F32) for c in widths)
    return pl.pallas_call(
        _in_proj_sample_kernel,
        grid=(1,),
        in_specs=[full(a) for a in args],
        out_specs=tuple(full(s) for s in out_shapes),
        out_shape=out_shapes,
        compiler_params=pltpu.CompilerParams(dimension_semantics=("arbitrary",),
                                             vmem_limit_bytes=VMEM_LIMIT),
        name="in_proj_sample",
    )(*args)


ATTN_TILE = 2048
assert all(ATTN_TILE % (WIN_KEYS * d) == 0 for d in DILATIONS)
ATTN_UNROLL = 8
assert (ATTN_TILE // WIN_KEYS) % ATTN_UNROLL == 0


def _attn_kernel(*refs, tile):
    n_pat = len(DILATIONS)
    qkv_refs = [refs[3 * p:3 * p + 3] for p in range(n_pat)]
    bias_ref, o_ref, acc_ref, m_ref, l_ref = refs[3 * n_pat:]
    t = pl.program_id(2)
    blk = WIN_KEYS
    n_blocks = tile // blk
    lane = lax.broadcasted_iota(jnp.int32, (1, LANES), 1)
    head_a = lane < HEAD_DIM

    def block_softmax(qb, k_ref, v_ref, r, q0):
        kk = pl.multiple_of(jnp.maximum(q0 - blk, 0), blk)
        bias = bias_ref[jnp.minimum(q0, 1)]
        kb = k_ref[r, pl.ds(kk, 2 * blk), :]
        vb = v_ref[r, pl.ds(kk, 2 * blk), :]
        res, mx = [], []
        for sel in (head_a, jnp.logical_not(head_a)):
            qh = jnp.where(sel, qb, jnp.zeros_like(qb))
            s = lax.dot_general(qh, kb, (((1,), (1,)), ((), ())), preferred_element_type=F32)
            s = s + bias
            m = jnp.max(s, axis=1, keepdims=True)
            p = jnp.exp2(s - m).astype(BF16)
            vh = jnp.where(sel, vb, jnp.ones_like(vb))
            res.append(jnp.dot(p, vh, preferred_element_type=F32))
            mx.append(m)
        num = jnp.where(head_a, res[0], res[1])
        den = pltpu.roll(jnp.where(head_a, res[1], res[0]), HEAD_DIM, 1)
        return num, den, jnp.where(head_a, mx[0], mx[1])

    order = sorted(range(n_pat), key=lambda p: -DILATIONS[p])
    for step, p in enumerate(order):
        d = DILATIONS[p]
        q_ref, k_ref, v_ref = qkv_refs[p]
        per_res = n_blocks // d
        assert per_res & (per_res - 1) == 0
        first, last = step == 0, step == n_pat - 1

        def body(it, carry, d=d, q_ref=q_ref, k_ref=k_ref, v_ref=v_ref, per_res=per_res,
                 first=first, last=last):
            for u in range(ATTN_UNROLL):
                i = it * ATTN_UNROLL + u
                r = lax.shift_right_logical(i, per_res.bit_length() - 1)
                jb = i & (per_res - 1)
                qb = q_ref[r, pl.ds(pl.multiple_of(jb * blk, blk), blk), :]
                num, den, mb = block_softmax(qb, k_ref, v_ref, r, t * (tile // d) + jb * blk)
                start = jb * blk * d + r
                rows = pl.ds(start, blk) if d == 1 else pl.ds(start, blk, stride=d)
                if not first:
                    m_old = m_ref[rows, :]
                    m_new = jnp.maximum(m_old, mb)
                    w_old = jnp.exp2(m_old - m_new)
                    w_blk = jnp.exp2(mb - m_new)
                    num = acc_ref[rows, :] * w_old + num * w_blk
                    den = l_ref[rows, :] * w_old + den * w_blk
                    mb = m_new
                if last:
                    o_ref[rows, :] = num / den
                else:
                    acc_ref[rows, :] = num
                    l_ref[rows, :] = den
                    m_ref[rows, :] = mb
            return carry

        lax.fori_loop(0, n_blocks // ATTN_UNROLL, body, 0)


def _band_bias():
    rel = np.arange(2 * WIN_KEYS)[None, :] - np.arange(WIN_KEYS)[:, None]
    variants = [rel - off for off in (0, WIN_KEYS)]
    return jnp.asarray(np.stack([np.where((v <= 0) & (v >= -WIN_KEYS), 0.0, NEG)
                                 for v in variants]), dtype=F32)


def _attention(qkv, b, s):
    tile = ATTN_TILE
    in_specs, args = [], []
    for d, (q, k, v) in zip(DILATIONS, qkv):
        qspec = pl.BlockSpec((None, None, d, tile // d, LANES), lambda bi, c, i: (bi, c, 0, i, 0))
        kvspec = pl.BlockSpec((None, None, d, s // d, LANES), lambda bi, c, i: (bi, c, 0, 0, 0))
        in_specs += [qspec, kvspec, kvspec]
        args += [q, k, v]
    bias = _band_bias()
    in_specs.append(pl.BlockSpec(bias.shape, lambda bi, c, i: (0, 0, 0)))
    args.append(bias)
    return pl.pallas_call(
        functools.partial(_attn_kernel, tile=tile),
        grid=(b, N_SLABS, s // tile),
        in_specs=in_specs,
        out_specs=pl.BlockSpec((None, tile, LANES), lambda bi, c, i: (bi, i, c)),
        out_shape=jax.ShapeDtypeStruct((b, s, D_ATTN), F32),
        scratch_shapes=[pltpu.VMEM((tile, LANES), F32)] * 3,
        compiler_params=pltpu.CompilerParams(
            dimension_semantics=("arbitrary", "arbitrary", "arbitrary"),
            vmem_limit_bytes=VMEM_LIMIT),
        name="dilated_attn",
    )(*args)


A_HALO = 32
C_HALO = 8


def _layernorm_rows(x, g, b):
    mu = jnp.mean(x, axis=-1, keepdims=True)
    xc = x - mu
    var = jnp.mean(xc * xc, axis=-1, keepdims=True)
    return xc * lax.rsqrt(var + EPS) * g + b


def _out_proj_kernel(x_ref, attn_ref, zs_ref, u_ref, uh_ref, gza_ref, bz_ref, uc_ref, uch_ref,
                     aw_ref, ab_ref, ag_ref, alb_ref, cw_ref, w_ref,
                     y_ref, ubuf, uphase, ucbuf, *, tm):
    first_tile = pl.program_id(1) == 0
    acc = jnp.dot((attn_ref[...] * zs_ref[...]).astype(BF16), w_ref[0:D_ATTN, :],
                  preferred_element_type=F32)

    ubuf[0:A_HALO, :] = jnp.where(first_tile, 0.0, uh_ref[...])
    ubuf[A_HALO:A_HALO + tm, :] = u_ref[...]
    n_rows = tm + A_HALO - SUBLANES
    for b in range(1, SUBLANES):
        uphase[b - 1] = ubuf[b:b + n_rows, :]
    base = A_HALO - (CONV_WIDTH - 1)
    groups = (tm // SUBLANES, SUBLANES, D_CONV)
    ca = jnp.zeros(groups, F32)
    for j in range(CONV_WIDTH):
        b = (base + j) % SUBLANES
        a8 = base + j - b
        rows = ubuf[a8:a8 + tm, :] if b == 0 else uphase[b - 1, a8:a8 + tm, :]
        ca = ca + aw_ref[j][None] * rows.reshape(groups)
    ca = ca.reshape(tm, D_CONV) + ab_ref[...]
    ya = _silu(_layernorm_rows(ca, ag_ref[...], alb_ref[...])) * gza_ref[...]
    acc = acc + jnp.dot(ya.astype(BF16), w_ref[D_ATTN:D_ATTN + D_CONV, :],
                        preferred_element_type=F32)

    ucbuf[0:C_HALO, :] = jnp.where(first_tile, 0.0, uch_ref[...])
    ucbuf[C_HALO:C_HALO + tm, :] = uc_ref[...]
    cbase = C_HALO - (SCONV_WIDTH - 1)
    cc = jnp.zeros((tm, D_SCONV), F32)
    for j in range(SCONV_WIDTH):
        cc = cc + cw_ref[j:j + 1, :] * ucbuf[cbase + j:cbase + j + tm, :]
    yc = bz_ref[...] * cc
    acc = acc + jnp.dot(yc.astype(BF16), w_ref[D_ATTN + D_CONV:, :],
                        preferred_element_type=F32)

    y_ref[...] = x_ref[...] + acc


def _out_proj(x, attn, zs, u, gza, bz, uc, aw, ab, ag, alb, cw, w_bf, tm):
    b, s, _ = x.shape
    tile = lambda c: pl.BlockSpec((None, tm, c), lambda bi, i: (bi, i, 0))
    halo = lambda h, c: pl.BlockSpec(
        (None, h, c), lambda bi, i: (bi, jnp.maximum(i * (tm // h) - 1, 0), 0))
    full = lambda a: pl.BlockSpec(a.shape, lambda bi, i: (0,) * a.ndim)
    return pl.pallas_call(
        functools.partial(_out_proj_kernel, tm=tm),
        grid=(b, s // tm),
        in_specs=[tile(D_MODEL), tile(D_ATTN), tile(D_ATTN),
                  tile(D_CONV), halo(A_HALO, D_CONV), tile(D_CONV), tile(D_SCONV),
                  tile(D_SCONV), halo(C_HALO, D_SCONV)]
                 + [full(aw), full(ab), full(ag), full(alb), full(cw), full(w_bf)],
        out_specs=tile(D_MODEL),
        out_shape=jax.ShapeDtypeStruct((b, s, D_MODEL), F32),
        scratch_shapes=[pltpu.VMEM((A_HALO + tm, D_CONV), F32),
                        pltpu.VMEM((SUBLANES - 1, A_HALO + tm - SUBLANES, D_CONV), F32),
                        pltpu.VMEM((C_HALO + tm, D_SCONV), F32)],
        compiler_params=pltpu.CompilerParams(dimension_semantics=("arbitrary", "arbitrary"),
                                             vmem_limit_bytes=VMEM_LIMIT),
        name="out_proj",
    )(x, attn, zs, u, u, gza, bz, uc, uc, aw, ab, ag, alb, cw, w_bf)


def _sample_attn_kernel(q_ref, kn_ref, vn_ref, kt_ref, vt_ref, attn_ref):
    l_cache = kt_ref.shape[1]
    kn = kn_ref[...]
    vn = vn_ref[...]
    head_of_lane = lax.broadcasted_iota(jnp.int32, (N_HEADS, D_ATTN), 1) // HEAD_DIM
    own = head_of_lane == lax.broadcasted_iota(jnp.int32, (N_HEADS, D_ATTN), 0)
    qblk = jnp.where(own, q_ref[...], 0.0)
    s = jnp.dot(qblk.astype(BF16), kt_ref[...].astype(BF16), preferred_element_type=F32)
    s_self = jnp.sum(qblk * kn, axis=1, keepdims=True)

    dist = l_cache - lax.broadcasted_iota(jnp.int32, (N_HEADS, l_cache), 1)
    cnt = jnp.zeros((N_HEADS, l_cache), F32)
    for w, d in zip(WINDOWS, DILATIONS):
        cnt = cnt + jnp.where((dist <= w) & ((dist & (d - 1)) == 0), 1.0, 0.0)
    s = jnp.where(cnt > 0, s, NEG)
    m = jnp.maximum(jnp.max(s, axis=1, keepdims=True), s_self)
    e = (cnt * jnp.exp(s - m)).astype(BF16)
    e_self = len(WINDOWS) * jnp.exp(s_self - m)
    den = jnp.sum(e.astype(F32), axis=1, keepdims=True) + e_self
    pv = lax.dot_general(e, vt_ref[...].astype(BF16), (((1,), (1,)), ((), ())),
                         preferred_element_type=F32)
    per_lane = lambda a: jnp.sum(jnp.where(own, a, 0.0), axis=0, keepdims=True)
    num = per_lane(pv) + per_lane(jnp.broadcast_to(e_self, own.shape)) * vn
    attn_ref[...] = num / per_lane(jnp.broadcast_to(den, own.shape))


def _sample_attn(q, kf, vf, cache_kt, cache_vt, layer):
    _, bd, _, l_cache = cache_kt.shape
    assert l_cache >= WINDOWS[-1]
    row = pl.BlockSpec((None, 1, D_ATTN), lambda b: (b, 0, 0))
    cache = pl.BlockSpec((None, None, D_ATTN, l_cache), lambda b: (layer, b, 0, 0))
    r3 = lambda a: a.reshape(bd, 1, D_ATTN)
    attn = pl.pallas_call(
        _sample_attn_kernel,
        grid=(bd,),
        in_specs=[row, row, row, cache, cache],
        out_specs=row,
        out_shape=jax.ShapeDtypeStruct((bd, 1, D_ATTN), F32),
        compiler_params=pltpu.CompilerParams(dimension_semantics=("arbitrary",),
                                             vmem_limit_bytes=VMEM_LIMIT),
        name="sample_attn",
    )(r3(q), r3(kf), r3(vf), cache_kt, cache_vt)
    return attn.reshape(bd, D_ATTN)


def _sample_out_kernel(x_ref, attn_ref, zs_ref, u_ref, gza_ref, bz_ref, uc_ref, sa_ref, sc_ref,
                       aw_ref, ab_ref, ag_ref, alb_ref, cw_ref, w_ref, y_ref, na_ref, nc_ref):
    u = u_ref[...]
    uc = uc_ref[...]
    na = CONV_WIDTH - 1
    nc = SCONV_WIDTH - 1
    ca = ab_ref[...] + aw_ref[na:na + 1, :] * u
    for j in range(na):
        ca = ca + aw_ref[j:j + 1, :] * sa_ref[:, j * D_CONV:(j + 1) * D_CONV]
    ya = _silu(_layernorm_rows(ca, ag_ref[...], alb_ref[...])) * gza_ref[...]
    cc = cw_ref[nc:nc + 1, :] * uc
    for j in range(nc):
        cc = cc + cw_ref[j:j + 1, :] * sc_ref[:, j * D_SCONV:(j + 1) * D_SCONV]
    yc = bz_ref[...] * cc
    acc = jnp.dot((attn_ref[...] * zs_ref[...]).astype(BF16), w_ref[0:D_ATTN, :],
                  preferred_element_type=F32)
    acc = acc + jnp.dot(ya.astype(BF16), w_ref[D_ATTN:D_ATTN + D_CONV, :],
                        preferred_element_type=F32)
    acc = acc + jnp.dot(yc.astype(BF16), w_ref[D_ATTN + D_CONV:, :],
                        preferred_element_type=F32)
    y_ref[...] = x_ref[...] + acc
    na_ref[:, 0:(na - 1) * D_CONV] = sa_ref[:, D_CONV:na * D_CONV]
    na_ref[:, (na - 1) * D_CONV:] = u
    nc_ref[:, 0:(nc - 1) * D_SCONV] = sc_ref[:, D_SCONV:nc * D_SCONV]
    nc_ref[:, (nc - 1) * D_SCONV:] = uc


def _sample_out(x, attn, zs, u, gza, bz, uc, sa, sc, aw, ab, ag, alb, cw, w_bf):
    bd = x.shape[0]
    sa2 = sa.reshape(bd, -1)
    sc2 = sc.reshape(bd, -1)
    args = (x, attn, zs, u, gza, bz, uc, sa2, sc2, aw, ab, ag, alb, cw, w_bf)
    full = lambda a: pl.BlockSpec(a.shape, lambda i: (0,) * a.ndim)
    y, na, nc = pl.pallas_call(
        _sample_out_kernel,
        grid=(1,),
        in_specs=[full(a) for a in args],
        out_specs=(full(x), full(sa2), full(sc2)),
        out_shape=(jax.ShapeDtypeStruct(x.shape, F32),
                   jax.ShapeDtypeStruct(sa2.shape, F32),
                   jax.ShapeDtypeStruct(sc2.shape, F32)),
        compiler_params=pltpu.CompilerParams(dimension_semantics=("arbitrary",),
                                             vmem_limit_bytes=VMEM_LIMIT),
        name="sample_out",
    )(*args)
    return y, na.reshape(sa.shape), nc.reshape(sc.shape)


def _rope_tables(pos):
    half = HEAD_DIM // 2
    inv = ROPE_THETA ** (-jnp.arange(half, dtype=F32) / half)
    ang = pos.astype(F32)[:, None] * inv[None, :]
    cos, sin = jnp.cos(ang), jnp.sin(ang)
    reps = LANES // HEAD_DIM
    cos_t = jnp.tile(jnp.concatenate([cos, cos], axis=-1), (1, reps))
    sin_t = jnp.tile(jnp.concatenate([-sin, sin], axis=-1), (1, reps))
    return cos_t, sin_t


def _segment_mean_matrix():
    seg = np.kron(np.eye(LANES // HEAD_DIM), np.ones((HEAD_DIM, HEAD_DIM))) / HEAD_DIM
    return jnp.asarray(seg, dtype=BF16)


def _to_channel_major(cache):
    lead = cache.shape[:-3]
    n = len(lead)
    perm = tuple(range(n)) + (n + 1, n + 2, n)
    return jnp.transpose(cache, perm).reshape(*lead, D_ATTN, cache.shape[-3])


def _from_channel_major(cache_t):
    lead = cache_t.shape[:-2]
    n = len(lead)
    perm = tuple(range(n)) + (n + 2, n, n + 1)
    return jnp.transpose(cache_t.reshape(*lead, N_HEADS, HEAD_DIM, cache_t.shape[-1]), perm)


PROMPT_TM_IN = 512
PROMPT_TM_OUT = 512


def kernel(x_prompt, x_sample, cache_k, cache_v, state_conv_a, state_conv_c, ln_g, w_in,
           q_norm_g, k_norm_g, a_conv_w, a_conv_b, a_ln_g, a_ln_b, c_conv_w, w_out):
    bp, sp, _ = x_prompt.shape
    bd, ts, _ = x_sample.shape
    assert ts == 1
    depth = w_in.shape[0]
    l_prompt = min(WINDOWS[-1], sp)

    cos_p, sin_p = _rope_tables(jnp.arange(sp, dtype=jnp.int32))
    cos_s, sin_s = _rope_tables(jnp.full((bd,), PAST_LEN, dtype=jnp.int32))
    seg = _segment_mean_matrix()
    row = lambda a: a.reshape(1, -1)
    tile_g = lambda g: jnp.tile(g, LANES // HEAD_DIM).reshape(1, LANES)
    cache_kt = _to_channel_major(cache_k)
    cache_vt = _to_channel_major(cache_v)

    w_in_bf = w_in.astype(BF16)
    w_out_bf = w_out.astype(BF16)
    layer_w = []
    for l in range(depth):
        conv_w = (a_conv_w[l], row(a_conv_b[l]), row(a_ln_g[l]), row(a_ln_b[l]), c_conv_w[l])
        layer_w.append((row(ln_g[l]), tile_g(q_norm_g[l]), tile_g(k_norm_g[l]), conv_w))

    hs = x_sample.reshape(bd, D_MODEL)
    sk, sv, sa, sc = [], [], [], []
    for l, (lng, qg, kg, conv_w) in enumerate(layer_w):
        q, kf, vf, zs, u, gza, bz, uc = _in_proj_sample(hs, lng, w_in_bf[l], cos_s, sin_s, qg, kg,
                                                        seg)
        attn = _sample_attn(q, kf, vf, cache_kt, cache_vt, l)
        hs, na, nc = _sample_out(hs, attn, zs, u, gza, bz, uc, state_conv_a[l], state_conv_c[l],
                                 *conv_w, w_out_bf[l])
        sk.append(kf)
        sv.append(vf)
        sa.append(na)
        sc.append(nc)
    new_cols = lambda rows: jnp.transpose(jnp.stack(rows), (0, 2, 1))

    assert depth >= 2
    shift_jobs = {depth - 2: (cache_kt, new_cols(sk)), depth - 1: (cache_vt, new_cols(sv))}
    shifted = {}

    hp = x_prompt
    pk, pv, pa, pc = [], [], [], []
    for l, (lng, qg, kg, conv_w) in enumerate(layer_w):
        outs = _in_proj_prompt(hp, lng, w_in_bf[l], cos_p, sin_p, qg, kg, seg, PROMPT_TM_IN,
                               l_prompt, shift=shift_jobs.get(l))
        if l in shift_jobs:
            outs, shifted[l] = outs[:-1], outs[-1]
        n_pat = len(DILATIONS)
        q_d, k_d, v_d = outs[0:n_pat], outs[n_pat:2 * n_pat], outs[2 * n_pat:3 * n_pat]
        kt, vt, zs, u, gza, bz, uc = outs[3 * n_pat:]
        attn = _attention(list(zip(q_d, k_d, v_d)), bp, sp)
        aw8 = jnp.broadcast_to(a_conv_w[l][:, None, :], (CONV_WIDTH, SUBLANES, D_CONV))
        hp = _out_proj(hp, attn, zs, u, gza, bz, uc, aw8, *conv_w[1:], w_out_bf[l],
                       PROMPT_TM_OUT)
        pk.append(kt)
        pv.append(vt)
        pa.append(u[:, sp - (CONV_WIDTH - 1):])
        pc.append(uc[:, sp - (SCONV_WIDTH - 1):])

    return (hp, hs.reshape(bd, ts, D_MODEL),
            _from_channel_major(jnp.stack(pk)), _from_channel_major(jnp.stack(pv)),
            jnp.stack(pa), jnp.stack(pc),
            _from_channel_major(shifted[depth - 2]), _from_channel_major(shifted[depth - 1]),
            jnp.stack(sa), jnp.stack(sc))
```

```python
import functools

import numpy as np
import jax
import jax.numpy as jnp
from jax import lax
from jax.experimental import pallas as pl
from jax.experimental.pallas import tpu as pltpu

D_MODEL = 1024
N_HEADS = 8
HEAD_DIM = 64
D_ATTN = N_HEADS * HEAD_DIM
D_CONV = 256
CONV_WIDTH = 31
D_SCONV = 256
SCONV_WIDTH = 3
WINDOWS = (128, 512, 2048)
DILATIONS = (1, 4, 16)
WIN_KEYS = 128
assert all(w // d == WIN_KEYS for w, d in zip(WINDOWS, DILATIONS))
assert all(d & (d - 1) == 0 for d in DILATIONS)
PAST_LEN = 16384
ROPE_THETA = 10000.0
EPS = 1e-6
NEG = -1e30
LOG2_E = 1.4426950408889634

C_Q, C_K, C_V, C_Z = 0, 512, 1024, 1536
C_AVAL, C_AGATE, C_ZA = 2048, 2304, 2560
C_CB, C_CC, C_CH, C_ZC = 2816, 3072, 3328, 3584
D_IN = 3840

LANES = 128
SUBLANES = 8
N_SLABS = D_ATTN // LANES
VMEM_LIMIT = 56 * 1024 * 1024

PROMPT_TM_IN = 512
PROMPT_TM_OUT = 512
KT_DILATIONS = tuple(d for d in DILATIONS if (PROMPT_TM_IN // d) % LANES == 0)

F32 = jnp.float32
BF16 = jnp.bfloat16


def _sigmoid(x):
    return 1.0 / (1.0 + jnp.exp(-x))


def _silu(x):
    return x * _sigmoid(x)


SHIFT_ROWS = 32


def _shift_cache_block(c_ref, nt_ref, n_ref, b):
    l_cache = c_ref.shape[1]
    bd = nt_ref.shape[1]
    mine = lax.broadcasted_iota(jnp.int32, (SHIFT_ROWS, bd), 1) == b
    last_lane = lax.broadcasted_iota(jnp.int32, (SHIFT_ROWS, LANES), 1) == LANES - 1
    for c in range(D_ATTN // SHIFT_ROWS):
        rows = slice(c * SHIFT_ROWS, (c + 1) * SHIFT_ROWS)
        new_col = jnp.sum(jnp.where(mine, nt_ref[rows, :], 0.0), axis=1, keepdims=True)
        rolled = pltpu.roll(c_ref[rows, :], l_cache - 1, 1)
        n_ref[rows, 0:l_cache - LANES] = rolled[:, 0:l_cache - LANES]
        n_ref[rows, l_cache - LANES:] = jnp.where(last_lane, new_col, rolled[:, l_cache - LANES:])


def _shift_specs(depth, bd, l_cache, step_of):
    cache = pl.BlockSpec((None, None, D_ATTN, l_cache),
                         lambda *g: (step_of(*g) // bd, step_of(*g) % bd, 0, 0))
    cols = pl.BlockSpec((None, D_ATTN, bd), lambda *g: (step_of(*g) // bd, 0, 0))
    return cache, cols


def _in_proj_body(x_ref, lng_ref, w_ref, cos_ref, sin_ref, qg_ref, kg_ref, seg_ref,
                  zs_ref, u_ref, gza_ref, bz_ref, uc_ref, emit_q, emit_k, emit_v, q_scale):
    x = x_ref[...]
    ms = jnp.mean(x * x, axis=-1, keepdims=True)
    xn = (x * lax.rsqrt(ms + EPS) * lng_ref[...]).astype(BF16)

    def proj(c0, c1):
        return jnp.dot(xn, w_ref[:, c0:c1], preferred_element_type=F32)

    cos = cos_ref[...]
    sin = sin_ref[...]
    tm = x.shape[0]
    first_half = (lax.broadcasted_iota(jnp.int32, (tm, LANES), 1) & (HEAD_DIM // 2)) == 0

    def norm_rope(p, g_ref, scale, emit):
        for s in range(N_SLABS):
            ps = p[:, s * LANES:(s + 1) * LANES]
            ss = jnp.dot((ps * ps).astype(BF16), seg_ref[...], preferred_element_type=F32)
            pn = ps * lax.rsqrt(ss + EPS) * g_ref[...]
            partner = jnp.where(first_half,
                                pltpu.roll(pn, LANES - HEAD_DIM // 2, 1),
                                pltpu.roll(pn, HEAD_DIM // 2, 1))
            emit(s, (pn * cos + partner * sin) * scale)

    norm_rope(proj(C_Q, C_K), qg_ref, q_scale, emit_q)
    norm_rope(proj(C_K, C_V), kg_ref, 1.0, emit_k)
    v = proj(C_V, C_Z)
    for s in range(N_SLABS):
        emit_v(s, v[:, s * LANES:(s + 1) * LANES])
    zs_ref[...] = _silu(proj(C_Z, C_AVAL)).astype(zs_ref.dtype)
    ag = proj(C_AVAL, C_ZA)
    u_ref[...] = ag[:, :D_CONV] * _sigmoid(ag[:, D_CONV:])
    zb = proj(C_ZA, C_CC)
    gza_ref[...] = _silu(zb[:, :D_CONV]).astype(gza_ref.dtype)
    zc = proj(C_ZC, D_IN)
    bz_ref[...] = (zb[:, D_CONV:] * _silu(zc)).astype(bz_ref.dtype)
    ch = proj(C_CC, C_ZC)
    uc_ref[...] = ch[:, :D_SCONV] * ch[:, D_SCONV:]


def _in_proj_prompt_kernel(x_ref, lng_ref, w_ref, cos_ref, sin_ref, qg_ref, kg_ref, seg_ref,
                           *rest, tm, shift):
    if shift:
        (c_ref, nt_ref, q1_ref, q4_ref, q16_ref, k1_ref, k4_ref, k16_ref, v1_ref, v4_ref, v16_ref,
         kt_ref, vt_ref, zs_ref, u_ref, gza_ref, bz_ref, uc_ref, n_ref, stage_ref) = rest
        grid_step = pl.program_id(0) * pl.num_programs(1) + pl.program_id(1)
        _shift_cache_block(c_ref, nt_ref, n_ref, grid_step % nt_ref.shape[1])
    else:
        (q1_ref, q4_ref, q16_ref, k1_ref, k4_ref, k16_ref, v1_ref, v4_ref, v16_ref,
         kt_ref, vt_ref, zs_ref, u_ref, gza_ref, bz_ref, uc_ref, stage_ref) = rest

    def emitter(which, outs, t_ref, transposed=()):
        def emit(s, val):
            stage_ref[which, s] = val
            val_t = val.T if t_ref is not None else None
            for d, out in zip(DILATIONS, outs):
                for r in range(d):
                    part = val if d == 1 else stage_ref[which, s, pl.ds(r, tm // d, stride=d), :]
                    if d in transposed:
                        part = val_t if d == 1 else part.T
                    out[s, r] = part.astype(BF16)
            if t_ref is not None:
                t_ref[s * LANES:(s + 1) * LANES, :] = val_t
        return emit

    _in_proj_body(x_ref, lng_ref, w_ref, cos_ref, sin_ref, qg_ref, kg_ref, seg_ref,
                  zs_ref, u_ref, gza_ref, bz_ref, uc_ref,
                  emitter(0, (q1_ref, q4_ref, q16_ref), None),
                  emitter(1, (k1_ref, k4_ref, k16_ref), kt_ref, transposed=KT_DILATIONS),
                  emitter(2, (v1_ref, v4_ref, v16_ref), vt_ref),
                  q_scale=HEAD_DIM ** -0.5 * LOG2_E)


def _in_proj_prompt(x, lng, w_bf, cos_t, sin_t, qg, kg, seg, tm, l_tail, shift=None):
    b, s, _ = x.shape
    grid = (b, s // tm)
    n_skip = (s - l_tail) // tm
    tile = lambda c: pl.BlockSpec((None, tm, c), lambda bi, i: (bi, i, 0))
    full = lambda a: pl.BlockSpec(a.shape, lambda bi, i: (0,) * a.ndim)
    pos = pl.BlockSpec((tm, LANES), lambda bi, i: (i, 0))
    def deint(d, transposed):
        blk = (LANES, tm // d) if transposed else (tm // d, LANES)
        idx = (lambda bi, i: (bi, 0, 0, 0, i)) if transposed else (lambda bi, i: (bi, 0, 0, i, 0))
        dims = (LANES, s // d) if transposed else (s // d, LANES)
        return (pl.BlockSpec((None, N_SLABS, d) + blk, idx),
                jax.ShapeDtypeStruct((b, N_SLABS, d) + dims, BF16))

    qkv = [deint(d, which == 1 and d in KT_DILATIONS) for which in range(3) for d in DILATIONS]
    tail = pl.BlockSpec((None, D_ATTN, tm), lambda bi, i: (bi, 0, jnp.maximum(i - n_skip, 0)))
    out_shapes = tuple(shape for _, shape in qkv) + (
        jax.ShapeDtypeStruct((b, D_ATTN, l_tail), F32),
        jax.ShapeDtypeStruct((b, D_ATTN, l_tail), F32),
        jax.ShapeDtypeStruct((b, s, D_ATTN), BF16),
        jax.ShapeDtypeStruct((b, s, D_CONV), F32),
        jax.ShapeDtypeStruct((b, s, D_CONV), BF16),
        jax.ShapeDtypeStruct((b, s, D_SCONV), BF16),
        jax.ShapeDtypeStruct((b, s, D_SCONV), F32),
    )
    out_specs = tuple(spec for spec, _ in qkv) + (
        tail, tail, tile(D_ATTN), tile(D_CONV), tile(D_CONV), tile(D_SCONV), tile(D_SCONV))
    w_spec = pl.BlockSpec(w_bf.shape, lambda bi, i: (0, 0), pipeline_mode=pl.Buffered(1))
    in_specs = [tile(D_MODEL), full(lng), w_spec, pos, pos, full(qg), full(kg), full(seg)]
    args = [x, lng, w_bf, cos_t, sin_t, qg, kg, seg]
    if shift is not None:
        cache, cols = shift
        depth, bd, _, l_cache = cache.shape
        assert depth * bd == grid[0] * grid[1]
        cache_spec, cols_spec = _shift_specs(depth, bd, l_cache, lambda bi, i: bi * grid[1] + i)
        in_specs += [cache_spec, cols_spec]
        args += [cache, cols]
        out_specs += (cache_spec,)
        out_shapes += (jax.ShapeDtypeStruct(cache.shape, F32),)
    return pl.pallas_call(
        functools.partial(_in_proj_prompt_kernel, tm=tm, shift=shift is not None),
        grid=grid,
        in_specs=in_specs,
        out_specs=out_specs,
        out_shape=out_shapes,
        scratch_shapes=[pltpu.VMEM((3, N_SLABS, tm, LANES), F32)],
        compiler_params=pltpu.CompilerParams(dimension_semantics=("arbitrary", "arbitrary"),
                                             vmem_limit_bytes=VMEM_LIMIT),
        name="in_proj",
    )(*args)


def _in_proj_sample_kernel(x_ref, lng_ref, w_ref, cos_ref, sin_ref, qg_ref, kg_ref, seg_ref,
                           q_ref, k_ref, v_ref, zs_ref, u_ref, gza_ref, bz_ref, uc_ref):
    def emitter(out):
        def emit(s, val):
            out[:, s * LANES:(s + 1) * LANES] = val
        return emit

    _in_proj_body(x_ref, lng_ref, w_ref, cos_ref, sin_ref, qg_ref, kg_ref, seg_ref,
                  zs_ref, u_ref, gza_ref, bz_ref, uc_ref,
                  emitter(q_ref), emitter(k_ref), emitter(v_ref), q_scale=HEAD_DIM ** -0.5)


def _in_proj_sample(x2d, lng, w_bf, cos_t, sin_t, qg, kg, seg):
    n = x2d.shape[0]
    args = (x2d, lng, w_bf, cos_t, sin_t, qg, kg, seg)
    full = lambda a: pl.BlockSpec(a.shape, lambda i: (0,) * a.ndim)
    widths = (D_ATTN, D_ATTN, D_ATTN, D_ATTN, D_CONV, D_CONV, D_SCONV, D_SCONV)
    out_shapes = tuple(jax.ShapeDtypeStruct((n, c), F32) for c in widths)
    return pl.pallas_call(
        _in_proj_sample_kernel,
        grid=(1,),
        in_specs=[full(a) for a in args],
        out_specs=tuple(full(s) for s in out_shapes),
        out_shape=out_shapes,
        compiler_params=pltpu.CompilerParams(dimension_semantics=("arbitrary",),
                                             vmem_limit_bytes=VMEM_LIMIT),
        name="in_proj_sample",
    )(*args)


ATTN_TILE = 2048
assert all(ATTN_TILE % (WIN_KEYS * d) == 0 for d in DILATIONS)
ATTN_UNROLL = 8
assert (ATTN_TILE // WIN_KEYS) % ATTN_UNROLL == 0


def _attn_kernel(*refs, tile):
    n_pat = len(DILATIONS)
    qkv_refs = [refs[3 * p:3 * p + 3] for p in range(n_pat)]
    bias_ref, o_ref, acc_ref, m_ref, l_ref = refs[3 * n_pat:]
    t = pl.program_id(2)
    blk = WIN_KEYS
    n_blocks = tile // blk
    lane = lax.broadcasted_iota(jnp.int32, (1, LANES), 1)
    head_a = lane < HEAD_DIM

    def block_softmax(qb, k_ref, v_ref, r, q0, k_channel_major):
        kk = pl.multiple_of(jnp.maximum(q0 - blk, 0), blk)
        bias = bias_ref[jnp.minimum(q0, 1)]
        if k_channel_major:
            kb = k_ref[r, :, pl.ds(kk, 2 * blk)]
            contract = (((1,), (0,)), ((), ()))
        else:
            kb = k_ref[r, pl.ds(kk, 2 * blk), :]
            contract = (((1,), (1,)), ((), ()))
        vb = v_ref[r, pl.ds(kk, 2 * blk), :]
        res, mx = [], []
        for sel in (head_a, jnp.logical_not(head_a)):
            qh = jnp.where(sel, qb, jnp.zeros_like(qb))
            s = lax.dot_general(qh, kb, contract, preferred_element_type=F32)
            s = s + bias
            m = jnp.max(s, axis=1, keepdims=True)
            p = jnp.exp2(s - m).astype(BF16)
            vh = jnp.where(sel, vb, jnp.ones_like(vb))
            res.append(jnp.dot(p, vh, preferred_element_type=F32))
            mx.append(m)
        num = jnp.where(head_a, res[0], res[1])
        den = pltpu.roll(jnp.where(head_a, res[1], res[0]), HEAD_DIM, 1)
        return num, den, jnp.where(head_a, mx[0], mx[1])

    order = sorted(range(n_pat), key=lambda p: -DILATIONS[p])
    for step, p in enumerate(order):
        d = DILATIONS[p]
        q_ref, k_ref, v_ref = qkv_refs[p]
        per_res = n_blocks // d
        assert per_res & (per_res - 1) == 0
        first, last = step == 0, step == n_pat - 1

        def body(it, carry, d=d, q_ref=q_ref, k_ref=k_ref, v_ref=v_ref, per_res=per_res,
                 first=first, last=last):
            for u in range(ATTN_UNROLL):
                i = it * ATTN_UNROLL + u
                r = lax.shift_right_logical(i, per_res.bit_length() - 1)
                jb = i & (per_res - 1)
                qb = q_ref[r, pl.ds(pl.multiple_of(jb * blk, blk), blk), :]
                num, den, mb = block_softmax(qb, k_ref, v_ref, r, t * (tile // d) + jb * blk,
                                             d in KT_DILATIONS)
                start = jb * blk * d + r
                rows = pl.ds(start, blk) if d == 1 else pl.ds(start, blk, stride=d)
                if not first:
                    m_old = m_ref[rows, :]
                    m_new = jnp.maximum(m_old, mb)
                    w_old = jnp.exp2(m_old - m_new)
                    w_blk = jnp.exp2(mb - m_new)
                    num = acc_ref[rows, :] * w_old + num * w_blk
                    den = l_ref[rows, :] * w_old + den * w_blk
                    mb = m_new
                if last:
                    o_ref[rows, :] = num / den
                else:
                    acc_ref[rows, :] = num
                    l_ref[rows, :] = den
                    m_ref[rows, :] = mb
            return carry

        lax.fori_loop(0, n_blocks // ATTN_UNROLL, body, 0)


def _band_bias():
    rel = np.arange(2 * WIN_KEYS)[None, :] - np.arange(WIN_KEYS)[:, None]
    variants = [rel - off for off in (0, WIN_KEYS)]
    return jnp.asarray(np.stack([np.where((v <= 0) & (v >= -WIN_KEYS), 0.0, NEG)
                                 for v in variants]), dtype=F32)


def _attention(qkv, b, s):
    tile = ATTN_TILE
    in_specs, args = [], []
    for d, (q, k, v) in zip(DILATIONS, qkv):
        qspec = pl.BlockSpec((None, None, d, tile // d, LANES), lambda bi, c, i: (bi, c, 0, i, 0))
        kvspec = pl.BlockSpec((None, None, d, s // d, LANES), lambda bi, c, i: (bi, c, 0, 0, 0))
        ktspec = pl.BlockSpec((None, None, d, LANES, s // d), lambda bi, c, i: (bi, c, 0, 0, 0))
        in_specs += [qspec, ktspec if d in KT_DILATIONS else kvspec, kvspec]
        args += [q, k, v]
    bias = _band_bias()
    in_specs.append(pl.BlockSpec(bias.shape, lambda bi, c, i: (0, 0, 0)))
    args.append(bias)
    return pl.pallas_call(
        functools.partial(_attn_kernel, tile=tile),
        grid=(b, N_SLABS, s // tile),
        in_specs=in_specs,
        out_specs=pl.BlockSpec((None, tile, LANES), lambda bi, c, i: (bi, i, c)),
        out_shape=jax.ShapeDtypeStruct((b, s, D_ATTN), F32),
        scratch_shapes=[pltpu.VMEM((tile, LANES), F32)] * 3,
        compiler_params=pltpu.CompilerParams(
            dimension_semantics=("arbitrary", "arbitrary", "arbitrary"),
            vmem_limit_bytes=VMEM_LIMIT),
        name="dilated_attn",
    )(*args)


A_HALO = 32
C_HALO = 8
OUT_CHUNK = 128


def _layernorm_rows(x, g, b):
    mu = jnp.mean(x, axis=-1, keepdims=True)
    xc = x - mu
    var = jnp.mean(xc * xc, axis=-1, keepdims=True)
    return xc * lax.rsqrt(var + EPS) * g + b


def _out_proj_kernel(x_ref, attn_ref, zs_ref, u_ref, uh_ref, gza_ref, bz_ref, uc_ref, uch_ref,
                     aw_ref, ab_ref, ag_ref, alb_ref, cw_ref, w_ref,
                     y_ref, ubuf, uphase, ucbuf, *, tm):
    first_tile = pl.program_id(1) == 0
    y_ref[...] = x_ref[...] + jnp.dot((attn_ref[...] * zs_ref[...]).astype(BF16),
                                      w_ref[0:D_ATTN, :], preferred_element_type=F32)

    ubuf[0:A_HALO, :] = jnp.where(first_tile, 0.0, uh_ref[...])
    ubuf[A_HALO:A_HALO + tm, :] = u_ref[...]
    n_rows = tm + A_HALO - SUBLANES
    for b in range(1, SUBLANES):
        uphase[b - 1] = ubuf[b:b + n_rows, :]
    ucbuf[0:C_HALO, :] = jnp.where(first_tile, 0.0, uch_ref[...])
    ucbuf[C_HALO:C_HALO + tm, :] = uc_ref[...]

    base = A_HALO - (CONV_WIDTH - 1)
    cbase = C_HALO - (SCONV_WIDTH - 1)
    groups = (OUT_CHUNK // SUBLANES, SUBLANES, D_CONV)
    for c in range(tm // OUT_CHUNK):
        r0 = c * OUT_CHUNK
        rows = slice(r0, r0 + OUT_CHUNK)
        ca = jnp.zeros(groups, F32)
        for j in range(CONV_WIDTH):
            b = (base + j) % SUBLANES
            a8 = base + j - b + r0
            src = (ubuf[a8:a8 + OUT_CHUNK, :] if b == 0
                   else uphase[b - 1, a8:a8 + OUT_CHUNK, :])
            ca = ca + aw_ref[j][None] * src.reshape(groups)
        ca = ca.reshape(OUT_CHUNK, D_CONV) + ab_ref[...]
        ya = _silu(_layernorm_rows(ca, ag_ref[...], alb_ref[...])) * gza_ref[rows, :]
        cc = jnp.zeros((OUT_CHUNK, D_SCONV), F32)
        for j in range(SCONV_WIDTH):
            cc = cc + cw_ref[j:j + 1, :] * ucbuf[cbase + j + r0:cbase + j + r0 + OUT_CHUNK, :]
        yc = bz_ref[rows, :] * cc
        y_ref[rows, :] += (
            jnp.dot(ya.astype(BF16), w_ref[D_ATTN:D_ATTN + D_CONV, :], preferred_element_type=F32)
            + jnp.dot(yc.astype(BF16), w_ref[D_ATTN + D_CONV:, :], preferred_element_type=F32))


def _out_proj(x, attn, zs, u, gza, bz, uc, aw, ab, ag, alb, cw, w_bf, tm):
    b, s, _ = x.shape
    tile = lambda c: pl.BlockSpec((None, tm, c), lambda bi, i: (bi, i, 0))
    halo = lambda h, c: pl.BlockSpec(
        (None, h, c), lambda bi, i: (bi, jnp.maximum(i * (tm // h) - 1, 0), 0))
    full = lambda a: pl.BlockSpec(a.shape, lambda bi, i: (0,) * a.ndim)
    return pl.pallas_call(
        functools.partial(_out_proj_kernel, tm=tm),
        grid=(b, s // tm),
        in_specs=[tile(D_MODEL), tile(D_ATTN), tile(D_ATTN),
                  tile(D_CONV), halo(A_HALO, D_CONV), tile(D_CONV), tile(D_SCONV),
                  tile(D_SCONV), halo(C_HALO, D_SCONV)]
                 + [full(aw), full(ab), full(ag), full(alb), full(cw), full(w_bf)],
        out_specs=tile(D_MODEL),
        out_shape=jax.ShapeDtypeStruct((b, s, D_MODEL), F32),
        scratch_shapes=[pltpu.VMEM((A_HALO + tm, D_CONV), F32),
                        pltpu.VMEM((SUBLANES - 1, A_HALO + tm - SUBLANES, D_CONV), F32),
                        pltpu.VMEM((C_HALO + tm, D_SCONV), F32)],
        compiler_params=pltpu.CompilerParams(dimension_semantics=("arbitrary", "arbitrary"),
                                             vmem_limit_bytes=VMEM_LIMIT),
        name="out_proj",
    )(x, attn, zs, u, u, gza, bz, uc, uc, aw, ab, ag, alb, cw, w_bf)


def _sample_attn_kernel(q_ref, kn_ref, vn_ref, kt_ref, vt_ref, attn_ref):
    l_cache = kt_ref.shape[1]
    kn = kn_ref[...]
    vn = vn_ref[...]
    head_of_lane = lax.broadcasted_iota(jnp.int32, (N_HEADS, D_ATTN), 1) // HEAD_DIM
    own = head_of_lane == lax.broadcasted_iota(jnp.int32, (N_HEADS, D_ATTN), 0)
    qblk = jnp.where(own, q_ref[...], 0.0)
    s = jnp.dot(qblk.astype(BF16), kt_ref[...].astype(BF16), preferred_element_type=F32)
    s_self = jnp.sum(qblk * kn, axis=1, keepdims=True)

    dist = l_cache - lax.broadcasted_iota(jnp.int32, (N_HEADS, l_cache), 1)
    cnt = jnp.zeros((N_HEADS, l_cache), F32)
    for w, d in zip(WINDOWS, DILATIONS):
        cnt = cnt + jnp.where((dist <= w) & ((dist & (d - 1)) == 0), 1.0, 0.0)
    s = jnp.where(cnt > 0, s, NEG)
    m = jnp.maximum(jnp.max(s, axis=1, keepdims=True), s_self)
    e = (cnt * jnp.exp(s - m)).astype(BF16)
    e_self = len(WINDOWS) * jnp.exp(s_self - m)
    den = jnp.sum(e.astype(F32), axis=1, keepdims=True) + e_self
    pv = lax.dot_general(e, vt_ref[...].astype(BF16), (((1,), (1,)), ((), ())),
                         preferred_element_type=F32)
    per_lane = lambda a: jnp.sum(jnp.where(own, a, 0.0), axis=0, keepdims=True)
    num = per_lane(pv) + per_lane(jnp.broadcast_to(e_self, own.shape)) * vn
    attn_ref[...] = num / per_lane(jnp.broadcast_to(den, own.shape))


def _sample_attn(q, kf, vf, cache_kt, cache_vt, layer):
    _, bd, _, l_cache = cache_kt.shape
    assert l_cache >= WINDOWS[-1]
    row = pl.BlockSpec((None, 1, D_ATTN), lambda b: (b, 0, 0))
    cache = pl.BlockSpec((None, None, D_ATTN, l_cache), lambda b: (layer, b, 0, 0))
    r3 = lambda a: a.reshape(bd, 1, D_ATTN)
    attn = pl.pallas_call(
        _sample_attn_kernel,
        grid=(bd,),
        in_specs=[row, row, row, cache, cache],
        out_specs=row,
        out_shape=jax.ShapeDtypeStruct((bd, 1, D_ATTN), F32),
        compiler_params=pltpu.CompilerParams(dimension_semantics=("arbitrary",),
                                             vmem_limit_bytes=VMEM_LIMIT),
        name="sample_attn",
    )(r3(q), r3(kf), r3(vf), cache_kt, cache_vt)
    return attn.reshape(bd, D_ATTN)


def _sample_out_kernel(x_ref, attn_ref, zs_ref, u_ref, gza_ref, bz_ref, uc_ref, sa_ref, sc_ref,
                       aw_ref, ab_ref, ag_ref, alb_ref, cw_ref, w_ref, y_ref, na_ref, nc_ref):
    u = u_ref[...]
    uc = uc_ref[...]
    na = CONV_WIDTH - 1
    nc = SCONV_WIDTH - 1
    ca = ab_ref[...] + aw_ref[na:na + 1, :] * u
    for j in range(na):
        ca = ca + aw_ref[j:j + 1, :] * sa_ref[:, j * D_CONV:(j + 1) * D_CONV]
    ya = _silu(_layernorm_rows(ca, ag_ref[...], alb_ref[...])) * gza_ref[...]
    cc = cw_ref[nc:nc + 1, :] * uc
    for j in range(nc):
        cc = cc + cw_ref[j:j + 1, :] * sc_ref[:, j * D_SCONV:(j + 1) * D_SCONV]
    yc = bz_ref[...] * cc
    acc = jnp.dot((attn_ref[...] * zs_ref[...]).astype(BF16), w_ref[0:D_ATTN, :],
                  preferred_element_type=F32)
    acc = acc + jnp.dot(ya.astype(BF16), w_ref[D_ATTN:D_ATTN + D_CONV, :],
                        preferred_element_type=F32)
    acc = acc + jnp.dot(yc.astype(BF16), w_ref[D_ATTN + D_CONV:, :],
                        preferred_element_type=F32)
    y_ref[...] = x_ref[...] + acc
    na_ref[:, 0:(na - 1) * D_CONV] = sa_ref[:, D_CONV:na * D_CONV]
    na_ref[:, (na - 1) * D_CONV:] = u
    nc_ref[:, 0:(nc - 1) * D_SCONV] = sc_ref[:, D_SCONV:nc * D_SCONV]
    nc_ref[:, (nc - 1) * D_SCONV:] = uc


def _sample_out(x, attn, zs, u, gza, bz, uc, sa, sc, aw, ab, ag, alb, cw, w_bf):
    bd = x.shape[0]
    sa2 = sa.reshape(bd, -1)
    sc2 = sc.reshape(bd, -1)
    args = (x, attn, zs, u, gza, bz, uc, sa2, sc2, aw, ab, ag, alb, cw, w_bf)
    full = lambda a: pl.BlockSpec(a.shape, lambda i: (0,) * a.ndim)
    y, na, nc = pl.pallas_call(
        _sample_out_kernel,
        grid=(1,),
        in_specs=[full(a) for a in args],
        out_specs=(full(x), full(sa2), full(sc2)),
        out_shape=(jax.ShapeDtypeStruct(x.shape, F32),
                   jax.ShapeDtypeStruct(sa2.shape, F32),
                   jax.ShapeDtypeStruct(sc2.shape, F32)),
        compiler_params=pltpu.CompilerParams(dimension_semantics=("arbitrary",),
                                             vmem_limit_bytes=VMEM_LIMIT),
        name="sample_out",
    )(*args)
    return y, na.reshape(sa.shape), nc.reshape(sc.shape)


def _rope_tables(pos):
    half = HEAD_DIM // 2
    inv = ROPE_THETA ** (-jnp.arange(half, dtype=F32) / half)
    ang = pos.astype(F32)[:, None] * inv[None, :]
    cos, sin = jnp.cos(ang), jnp.sin(ang)
    reps = LANES // HEAD_DIM
    cos_t = jnp.tile(jnp.concatenate([cos, cos], axis=-1), (1, reps))
    sin_t = jnp.tile(jnp.concatenate([-sin, sin], axis=-1), (1, reps))
    return cos_t, sin_t


def _segment_mean_matrix():
    seg = np.kron(np.eye(LANES // HEAD_DIM), np.ones((HEAD_DIM, HEAD_DIM))) / HEAD_DIM
    return jnp.asarray(seg, dtype=BF16)


def _to_channel_major(cache):
    lead = cache.shape[:-3]
    n = len(lead)
    perm = tuple(range(n)) + (n + 1, n + 2, n)
    return jnp.transpose(cache, perm).reshape(*lead, D_ATTN, cache.shape[-3])


def _from_channel_major(cache_t):
    lead = cache_t.shape[:-2]
    n = len(lead)
    perm = tuple(range(n)) + (n + 2, n, n + 1)
    return jnp.transpose(cache_t.reshape(*lead, N_HEADS, HEAD_DIM, cache_t.shape[-1]), perm)


def kernel(x_prompt, x_sample, cache_k, cache_v, state_conv_a, state_conv_c, ln_g, w_in,
           q_norm_g, k_norm_g, a_conv_w, a_conv_b, a_ln_g, a_ln_b, c_conv_w, w_out):
    bp, sp, _ = x_prompt.shape
    bd, ts, _ = x_sample.shape
    assert ts == 1
    depth = w_in.shape[0]
    l_prompt = min(WINDOWS[-1], sp)

    cos_p, sin_p = _rope_tables(jnp.arange(sp, dtype=jnp.int32))
    cos_s, sin_s = _rope_tables(jnp.full((bd,), PAST_LEN, dtype=jnp.int32))
    seg = _segment_mean_matrix()
    row = lambda a: a.reshape(1, -1)
    tile_g = lambda g: jnp.tile(g, LANES // HEAD_DIM).reshape(1, LANES)
    cache_kt = _to_channel_major(cache_k)
    cache_vt = _to_channel_major(cache_v)

    w_in_bf = w_in.astype(BF16)
    w_out_bf = w_out.astype(BF16)
    layer_w = []
    for l in range(depth):
        conv_w = (a_conv_w[l], row(a_conv_b[l]), row(a_ln_g[l]), row(a_ln_b[l]), c_conv_w[l])
        layer_w.append((row(ln_g[l]), tile_g(q_norm_g[l]), tile_g(k_norm_g[l]), conv_w))

    hs = x_sample.reshape(bd, D_MODEL)
    sk, sv, sa, sc = [], [], [], []
    for l, (lng, qg, kg, conv_w) in enumerate(layer_w):
        q, kf, vf, zs, u, gza, bz, uc = _in_proj_sample(hs, lng, w_in_bf[l], cos_s, sin_s, qg, kg,
                                                        seg)
        attn = _sample_attn(q, kf, vf, cache_kt, cache_vt, l)
        hs, na, nc = _sample_out(hs, attn, zs, u, gza, bz, uc, state_conv_a[l], state_conv_c[l],
                                 *conv_w, w_out_bf[l])
        sk.append(kf)
        sv.append(vf)
        sa.append(na)
        sc.append(nc)
    new_cols = lambda rows: jnp.transpose(jnp.stack(rows), (0, 2, 1))

    assert depth >= 2
    shift_jobs = {depth - 2: (cache_kt, new_cols(sk)), depth - 1: (cache_vt, new_cols(sv))}
    shifted = {}

    hp = x_prompt
    pk, pv, pa, pc = [], [], [], []
    for l, (lng, qg, kg, conv_w) in enumerate(layer_w):
        outs = _in_proj_prompt(hp, lng, w_in_bf[l], cos_p, sin_p, qg, kg, seg, PROMPT_TM_IN,
                               l_prompt, shift=shift_jobs.get(l))
        if l in shift_jobs:
            outs, shifted[l] = outs[:-1], outs[-1]
        n_pat = len(DILATIONS)
        q_d, k_d, v_d = outs[0:n_pat], outs[n_pat:2 * n_pat], outs[2 * n_pat:3 * n_pat]
        kt, vt, zs, u, gza, bz, uc = outs[3 * n_pat:]
        attn = _attention(list(zip(q_d, k_d, v_d)), bp, sp)
        aw8 = jnp.broadcast_to(a_conv_w[l][:, None, :], (CONV_WIDTH, SUBLANES, D_CONV))
        hp = _out_proj(hp, attn, zs, u, gza, bz, uc, aw8, *conv_w[1:], w_out_bf[l],
                       PROMPT_TM_OUT)
        pk.append(kt)
        pv.append(vt)
        pa.append(u[:, sp - (CONV_WIDTH - 1):])
        pc.append(uc[:, sp - (SCONV_WIDTH - 1):])

    return (hp, hs.reshape(bd, ts, D_MODEL),
            _from_channel_major(jnp.stack(pk)), _from_channel_major(jnp.stack(pv)),
            jnp.stack(pa), jnp.stack(pc),
            _from_channel_major(shifted[depth - 2]), _from_channel_major(shifted[depth - 1]),
            jnp.stack(sa), jnp.stack(sc))
```

```python
import functools

import numpy as np
import jax
import jax.numpy as jnp
from jax import lax
from jax.experimental import pallas as pl
from jax.experimental.pallas import tpu as pltpu

D_MODEL = 1024
N_HEADS = 8
HEAD_DIM = 64
D_ATTN = N_HEADS * HEAD_DIM
D_CONV = 256
CONV_WIDTH = 31
D_SCONV = 256
SCONV_WIDTH = 3
WINDOWS = (128, 512, 2048)
DILATIONS = (1, 4, 16)
WIN_KEYS = 128
assert all(w // d == WIN_KEYS for w, d in zip(WINDOWS, DILATIONS))
assert all(d & (d - 1) == 0 for d in DILATIONS)
assert len(DILATIONS) == 3 and DILATIONS[0] == 1 and DILATIONS[2] % DILATIONS[1] == 0
PAST_LEN = 16384
ROPE_THETA = 10000.0
EPS = 1e-6
NEG = -1e30
LOG2_E = 1.4426950408889634

C_Q, C_K, C_V, C_Z = 0, 512, 1024, 1536
C_AVAL, C_AGATE, C_ZA = 2048, 2304, 2560
C_CB, C_CC, C_CH, C_ZC = 2816, 3072, 3328, 3584
D_IN = 3840

LANES = 128
SUBLANES = 8
N_SLABS = D_ATTN // LANES
VMEM_LIMIT = 56 * 1024 * 1024

PROMPT_TM_IN = 512
PROMPT_TM_OUT = 512
KT_DILATIONS = tuple(d for d in DILATIONS if (PROMPT_TM_IN // d) % LANES == 0)

F32 = jnp.float32
BF16 = jnp.bfloat16


def _sigmoid(x):
    return 1.0 / (1.0 + jnp.exp(-x))


def _silu(x):
    return x * _sigmoid(x)


SHIFT_ROWS = 32


def _shift_cache_block(c_ref, nt_ref, n_ref, b):
    n_rows, l_cache = c_ref.shape
    bd = nt_ref.shape[1]
    mine = lax.broadcasted_iota(jnp.int32, (SHIFT_ROWS, bd), 1) == b
    last_lane = lax.broadcasted_iota(jnp.int32, (SHIFT_ROWS, LANES), 1) == LANES - 1
    for c in range(n_rows // SHIFT_ROWS):
        rows = slice(c * SHIFT_ROWS, (c + 1) * SHIFT_ROWS)
        new_col = jnp.sum(jnp.where(mine, nt_ref[rows, :], 0.0), axis=1, keepdims=True)
        rolled = pltpu.roll(c_ref[rows, :], l_cache - 1, 1)
        n_ref[rows, 0:l_cache - LANES] = rolled[:, 0:l_cache - LANES]
        n_ref[rows, l_cache - LANES:] = jnp.where(last_lane, new_col, rolled[:, l_cache - LANES:])


def _shift_specs(depth, bd, l_cache, step_of):
    cache = pl.BlockSpec((None, None, D_ATTN, l_cache),
                         lambda *g: (step_of(*g) // bd, step_of(*g) % bd, 0, 0))
    cols = pl.BlockSpec((None, D_ATTN, bd), lambda *g: (step_of(*g) // bd, 0, 0))
    return cache, cols


def _in_proj_body(x_ref, lng_ref, w_ref, cos_ref, sin_ref, qg_ref, kg_ref, seg_ref,
                  emit_q, emit_k, emit_v, emit_rest, q_scale):
    x = x_ref[...]
    ms = jnp.mean(x * x, axis=-1, keepdims=True)
    xn = (x * lax.rsqrt(ms + EPS) * lng_ref[...]).astype(BF16)

    def proj(c0, c1):
        return jnp.dot(xn, w_ref[:, c0:c1], preferred_element_type=F32)

    cos = cos_ref[...]
    sin = sin_ref[...]
    tm = x.shape[0]
    first_half = (lax.broadcasted_iota(jnp.int32, (tm, LANES), 1) & (HEAD_DIM // 2)) == 0

    def norm_rope(p, g_ref, scale, emit):
        for s in range(N_SLABS):
            ps = p[:, s * LANES:(s + 1) * LANES]
            ss = jnp.dot((ps * ps).astype(BF16), seg_ref[...], preferred_element_type=F32)
            pn = ps * lax.rsqrt(ss + EPS) * g_ref[...]
            partner = jnp.where(first_half,
                                pltpu.roll(pn, LANES - HEAD_DIM // 2, 1),
                                pltpu.roll(pn, HEAD_DIM // 2, 1))
            emit(s, (pn * cos + partner * sin) * scale)

    norm_rope(proj(C_Q, C_K), qg_ref, q_scale, emit_q)
    norm_rope(proj(C_K, C_V), kg_ref, 1.0, emit_k)
    v = proj(C_V, C_Z)
    for s in range(N_SLABS):
        emit_v(s, v[:, s * LANES:(s + 1) * LANES])
    zs = _silu(proj(C_Z, C_AVAL))
    ag = proj(C_AVAL, C_ZA)
    u = ag[:, :D_CONV] * _sigmoid(ag[:, D_CONV:])
    zb = proj(C_ZA, C_CC)
    zc = proj(C_ZC, D_IN)
    ch = proj(C_CC, C_ZC)
    emit_rest(zs, u, _silu(zb[:, :D_CONV]), zb[:, D_CONV:] * _silu(zc),
              ch[:, :D_SCONV] * ch[:, D_SCONV:])


A_HALO = 32
C_HALO = 8
CONV_CHUNK = 128


def _layernorm_rows(x, g, b):
    mu = jnp.mean(x, axis=-1, keepdims=True)
    xc = x - mu
    var = jnp.mean(xc * xc, axis=-1, keepdims=True)
    return xc * lax.rsqrt(var + EPS) * g + b


def _in_proj_prompt_kernel(x_ref, lng_ref, w_ref, cos_ref, sin_ref, qg_ref, kg_ref, seg_ref,
                           aw_ref, ab_ref, ag_ref, alb_ref, cw_ref, *rest, tm, shift):
    if shift:
        c_ref, nt_ref = rest[:2]
        rest = rest[2:]
    (q1_ref, q4_ref, q16_ref, k1_ref, k4_ref, k16_ref, v1_ref, v4_ref, v16_ref,
     kt_ref, vt_ref, zs_ref, ya_ref, yc_ref, ut_ref, uct_ref) = rest[:16]
    stage_ref, stage_next_ref, ubuf, uphase, ucbuf = rest[-5:]

    first_tile = pl.program_id(1) == 0

    @pl.when(first_tile)
    def _():
        ubuf[0:A_HALO, :] = jnp.zeros((A_HALO, D_CONV), F32)
        ucbuf[0:C_HALO, :] = jnp.zeros((C_HALO, D_SCONV), F32)

    @pl.when(jnp.logical_not(first_tile))
    def _():
        ubuf[0:A_HALO, :] = ubuf[tm:tm + A_HALO, :]
        ucbuf[0:C_HALO, :] = ucbuf[tm:tm + C_HALO, :]

    if shift:
        grid_step = pl.program_id(0) * pl.num_programs(1) + pl.program_id(1)
        _shift_cache_block(c_ref, nt_ref, rest[16], grid_step % nt_ref.shape[1])

    def emit_rest(zs, u, gza, bz, uc):
        zs_ref[...] = zs.astype(BF16)

        ubuf[A_HALO:A_HALO + tm, :] = u
        ut_ref[...] = u[tm - A_HALO:, :]
        n_rows = tm + A_HALO - SUBLANES
        for b in range(1, SUBLANES):
            uphase[b - 1] = ubuf[b:b + n_rows, :]
        base = A_HALO - (CONV_WIDTH - 1)
        groups = (CONV_CHUNK // SUBLANES, SUBLANES, D_CONV)
        for c in range(tm // CONV_CHUNK):
            r0 = c * CONV_CHUNK
            ca = jnp.zeros(groups, F32)
            for j in range(CONV_WIDTH):
                b = (base + j) % SUBLANES
                a8 = base + j - b + r0
                src = (ubuf[a8:a8 + CONV_CHUNK, :] if b == 0
                       else uphase[b - 1, a8:a8 + CONV_CHUNK, :])
                ca = ca + aw_ref[j][None] * src.reshape(groups)
            ca = ca.reshape(CONV_CHUNK, D_CONV) + ab_ref[...]
            ya = _silu(_layernorm_rows(ca, ag_ref[...], alb_ref[...])) * gza[r0:r0 + CONV_CHUNK]
            ya_ref[r0:r0 + CONV_CHUNK, :] = ya.astype(BF16)

        ucbuf[C_HALO:C_HALO + tm, :] = uc
        uct_ref[...] = uc[tm - C_HALO:, :]
        cbase = C_HALO - (SCONV_WIDTH - 1)
        cc = jnp.zeros((tm, D_SCONV), F32)
        for j in range(SCONV_WIDTH):
            cc = cc + cw_ref[j:j + 1, :] * ucbuf[cbase + j:cbase + j + tm, :]
        yc_ref[...] = (bz * cc).astype(BF16)

    def emitter(which, outs, t_ref, transposed=()):
        def emit(s, val):
            slot = (which * N_SLABS + s) % 2
            levels = {1: stage_ref.at[slot]}
            levels[1][0] = val
            val_t = val.T if t_ref is not None else None
            for li, (d, out) in enumerate(zip(DILATIONS, outs)):
                prev = DILATIONS[li - 1] if li else 1
                keep = li + 1 < len(DILATIONS)
                for r in range(d):
                    if d == 1:
                        part = val
                    else:
                        part = levels[prev][r % prev, pl.ds(r // prev, tm // d, stride=d // prev), :]
                        if keep:
                            stage_next_ref[slot, r] = part
                    if d in transposed:
                        part = val_t if d == 1 else part.T
                    out[s, r] = part.astype(BF16)
                if d != 1 and keep:
                    levels[d] = stage_next_ref.at[slot]
            if t_ref is not None:
                t_ref[s * LANES:(s + 1) * LANES, :] = val_t
        return emit

    _in_proj_body(x_ref, lng_ref, w_ref, cos_ref, sin_ref, qg_ref, kg_ref, seg_ref,
                  emitter(0, (q1_ref, q4_ref, q16_ref), None),
                  emitter(1, (k1_ref, k4_ref, k16_ref), kt_ref, transposed=KT_DILATIONS),
                  emitter(2, (v1_ref, v4_ref, v16_ref), vt_ref),
                  emit_rest,
                  q_scale=HEAD_DIM ** -0.5 * LOG2_E)


def _in_proj_prompt(x, lng, w_bf, cos_t, sin_t, qg, kg, seg, conv_w, tm, l_tail, shift=None):
    b, s, _ = x.shape
    grid = (b, s // tm)
    n_skip = (s - l_tail) // tm
    tile = lambda c: pl.BlockSpec((None, tm, c), lambda bi, i: (bi, i, 0))
    full = lambda a: pl.BlockSpec(a.shape, lambda bi, i: (0,) * a.ndim)
    pos = pl.BlockSpec((tm, LANES), lambda bi, i: (i, 0))
    def deint(d, transposed):
        blk = (LANES, tm // d) if transposed else (tm // d, LANES)
        idx = (lambda bi, i: (bi, 0, 0, 0, i)) if transposed else (lambda bi, i: (bi, 0, 0, i, 0))
        dims = (LANES, s // d) if transposed else (s // d, LANES)
        return (pl.BlockSpec((None, N_SLABS, d) + blk, idx),
                jax.ShapeDtypeStruct((b, N_SLABS, d) + dims, BF16))

    qkv = [deint(d, which == 1 and d in KT_DILATIONS) for which in range(3) for d in DILATIONS]
    tail = pl.BlockSpec((None, D_ATTN, tm), lambda bi, i: (bi, 0, jnp.maximum(i - n_skip, 0)))
    last_rows = lambda h, c: pl.BlockSpec((None, h, c), lambda bi, i: (bi, 0, 0))
    out_shapes = tuple(shape for _, shape in qkv) + (
        jax.ShapeDtypeStruct((b, D_ATTN, l_tail), F32),
        jax.ShapeDtypeStruct((b, D_ATTN, l_tail), F32),
        jax.ShapeDtypeStruct((b, s, D_ATTN), BF16),
        jax.ShapeDtypeStruct((b, s, D_CONV), BF16),
        jax.ShapeDtypeStruct((b, s, D_SCONV), BF16),
        jax.ShapeDtypeStruct((b, A_HALO, D_CONV), F32),
        jax.ShapeDtypeStruct((b, C_HALO, D_SCONV), F32),
    )
    out_specs = tuple(spec for spec, _ in qkv) + (
        tail, tail, tile(D_ATTN), tile(D_CONV), tile(D_SCONV),
        last_rows(A_HALO, D_CONV), last_rows(C_HALO, D_SCONV))
    w_spec = pl.BlockSpec(w_bf.shape, lambda bi, i: (0, 0), pipeline_mode=pl.Buffered(1))
    in_specs = ([tile(D_MODEL), full(lng), w_spec, pos, pos, full(qg), full(kg), full(seg)]
                + [full(a) for a in conv_w])
    args = [x, lng, w_bf, cos_t, sin_t, qg, kg, seg, *conv_w]
    if shift is not None:
        cache, cols = shift
        depth, bd, _, l_cache = cache.shape
        assert depth * bd == grid[0] * grid[1]
        cache_spec, cols_spec = _shift_specs(depth, bd, l_cache, lambda bi, i: bi * grid[1] + i)
        in_specs += [cache_spec, cols_spec]
        args += [cache, cols]
        out_specs += (cache_spec,)
        out_shapes += (jax.ShapeDtypeStruct(cache.shape, F32),)
    return pl.pallas_call(
        functools.partial(_in_proj_prompt_kernel, tm=tm, shift=shift is not None),
        grid=grid,
        in_specs=in_specs,
        out_specs=out_specs,
        out_shape=out_shapes,
        scratch_shapes=[
            pltpu.VMEM((2, 1, tm, LANES), F32),
            pltpu.VMEM((2, DILATIONS[1], tm // DILATIONS[1], LANES), F32),
            pltpu.VMEM((A_HALO + tm, D_CONV), F32),
            pltpu.VMEM((SUBLANES - 1, A_HALO + tm - SUBLANES, D_CONV), F32),
            pltpu.VMEM((C_HALO + tm, D_SCONV), F32)],
        compiler_params=pltpu.CompilerParams(dimension_semantics=("arbitrary", "arbitrary"),
                                             vmem_limit_bytes=VMEM_LIMIT),
        name="in_proj",
    )(*args)


def _in_proj_sample_kernel(x_ref, lng_ref, w_ref, cos_ref, sin_ref, qg_ref, kg_ref, seg_ref,
                           q_ref, k_ref, v_ref, zs_ref, u_ref, gza_ref, bz_ref, uc_ref):
    def emitter(out):
        def emit(s, val):
            out[:, s * LANES:(s + 1) * LANES] = val
        return emit

    def emit_rest(zs, u, gza, bz, uc):
        zs_ref[...] = zs
        u_ref[...] = u
        gza_ref[...] = gza
        bz_ref[...] = bz
        uc_ref[...] = uc

    _in_proj_body(x_ref, lng_ref, w_ref, cos_ref, sin_ref, qg_ref, kg_ref, seg_ref,
                  emitter(q_ref), emitter(k_ref), emitter(v_ref), emit_rest,
                  q_scale=HEAD_DIM ** -0.5)


def _in_proj_sample(x2d, lng, w_bf, cos_t, sin_t, qg, kg, seg):
    n = x2d.shape[0]
    args = (x2d, lng, w_bf, cos_t, sin_t, qg, kg, seg)
    full = lambda a: pl.BlockSpec(a.shape, lambda i: (0,) * a.ndim)
    widths = (D_ATTN, D_ATTN, D_ATTN, D_ATTN, D_CONV, D_CONV, D_SCONV, D_SCONV)
    out_shapes = tuple(jax.ShapeDtypeStruct((n, c), F32) for c in widths)
    return pl.pallas_call(
        _in_proj_sample_kernel,
        grid=(1,),
        in_specs=[full(a) for a in args],
        out_specs=tuple(full(s) for s in out_shapes),
        out_shape=out_shapes,
        compiler_params=pltpu.CompilerParams(dimension_semantics=("arbitrary",),
                                             vmem_limit_bytes=VMEM_LIMIT),
        name="in_proj_sample",
    )(*args)


ATTN_TILE = 2048
assert all(ATTN_TILE % (WIN_KEYS * d) == 0 for d in DILATIONS)
ATTN_UNROLL = 8
assert (ATTN_TILE // WIN_KEYS) % ATTN_UNROLL == 0


def _attn_kernel(*refs, tile):
    n_pat = len(DILATIONS)
    qkv_refs = [refs[3 * p:3 * p + 3] for p in range(n_pat)]
    bias_ref, o_ref, acc_ref, m_ref, l_ref = refs[3 * n_pat:]
    t = pl.program_id(2)
    blk = WIN_KEYS
    n_blocks = tile // blk
    lane = lax.broadcasted_iota(jnp.int32, (1, LANES), 1)
    head_a = lane < HEAD_DIM

    def block_softmax(qb, k_ref, v_ref, r, q0, k_channel_major):
        kk = pl.multiple_of(jnp.maximum(q0 - blk, 0), blk)
        bias = bias_ref[jnp.minimum(q0, 1)]
        if k_channel_major:
            kb = k_ref[r, :, pl.ds(kk, 2 * blk)]
            contract = (((1,), (0,)), ((), ()))
        else:
            kb = k_ref[r, pl.ds(kk, 2 * blk), :]
            contract = (((1,), (1,)), ((), ()))
        vb = v_ref[r, pl.ds(kk, 2 * blk), :]
        res, mx = [], []
        for sel in (head_a, jnp.logical_not(head_a)):
            qh = jnp.where(sel, qb, jnp.zeros_like(qb))
            s = lax.dot_general(qh, kb, contract, preferred_element_type=F32)
            s = s + bias
            m = jnp.max(s, axis=1, keepdims=True)
            p = jnp.exp2(s - m).astype(BF16)
            vh = jnp.where(sel, vb, jnp.ones_like(vb))
            res.append(jnp.dot(p, vh, preferred_element_type=F32))
            mx.append(m)
        num = jnp.where(head_a, res[0], res[1])
        den = pltpu.roll(jnp.where(head_a, res[1], res[0]), HEAD_DIM, 1)
        return num, den, jnp.where(head_a, mx[0], mx[1])

    order = sorted(range(n_pat), key=lambda p: -DILATIONS[p])
    for step, p in enumerate(order):
        d = DILATIONS[p]
        q_ref, k_ref, v_ref = qkv_refs[p]
        per_res = n_blocks // d
        assert per_res & (per_res - 1) == 0
        first, last = step == 0, step == n_pat - 1

        def body(it, carry, d=d, q_ref=q_ref, k_ref=k_ref, v_ref=v_ref, per_res=per_res,
                 first=first, last=last):
            for u in range(ATTN_UNROLL):
                i = it * ATTN_UNROLL + u
                r = lax.shift_right_logical(i, per_res.bit_length() - 1)
                jb = i & (per_res - 1)
                qb = q_ref[r, pl.ds(pl.multiple_of(jb * blk, blk), blk), :]
                num, den, mb = block_softmax(qb, k_ref, v_ref, r, t * (tile // d) + jb * blk,
                                             d in KT_DILATIONS)
                start = jb * blk * d + r
                rows = pl.ds(start, blk) if d == 1 else pl.ds(start, blk, stride=d)
                if not first:
                    m_old = m_ref[rows, :]
                    m_new = jnp.maximum(m_old, mb)
                    w_old = jnp.exp2(m_old - m_new)
                    w_blk = jnp.exp2(mb - m_new)
                    num = acc_ref[rows, :] * w_old + num * w_blk
                    den = l_ref[rows, :] * w_old + den * w_blk
                    mb = m_new
                if last:
                    o_ref[rows, :] = num / den
                else:
                    acc_ref[rows, :] = num
                    l_ref[rows, :] = den
                    m_ref[rows, :] = mb
            return carry

        lax.fori_loop(0, n_blocks // ATTN_UNROLL, body, 0)


def _band_bias():
    rel = np.arange(2 * WIN_KEYS)[None, :] - np.arange(WIN_KEYS)[:, None]
    variants = [rel - off for off in (0, WIN_KEYS)]
    return jnp.asarray(np.stack([np.where((v <= 0) & (v >= -WIN_KEYS), 0.0, NEG)
                                 for v in variants]), dtype=F32)


def _attention(qkv, b, s):
    tile = ATTN_TILE
    in_specs, args = [], []
    for d, (q, k, v) in zip(DILATIONS, qkv):
        qspec = pl.BlockSpec((None, None, d, tile // d, LANES), lambda bi, c, i: (bi, c, 0, i, 0))
        kvspec = pl.BlockSpec((None, None, d, s // d, LANES), lambda bi, c, i: (bi, c, 0, 0, 0))
        ktspec = pl.BlockSpec((None, None, d, LANES, s // d), lambda bi, c, i: (bi, c, 0, 0, 0))
        in_specs += [qspec, ktspec if d in KT_DILATIONS else kvspec, kvspec]
        args += [q, k, v]
    bias = _band_bias()
    in_specs.append(pl.BlockSpec(bias.shape, lambda bi, c, i: (0, 0, 0)))
    args.append(bias)
    return pl.pallas_call(
        functools.partial(_attn_kernel, tile=tile),
        grid=(b, N_SLABS, s // tile),
        in_specs=in_specs,
        out_specs=pl.BlockSpec((None, tile, LANES), lambda bi, c, i: (bi, i, c)),
        out_shape=jax.ShapeDtypeStruct((b, s, D_ATTN), F32),
        scratch_shapes=[pltpu.VMEM((tile, LANES), F32)] * 3,
        compiler_params=pltpu.CompilerParams(
            dimension_semantics=("arbitrary", "arbitrary", "arbitrary"),
            vmem_limit_bytes=VMEM_LIMIT),
        name="dilated_attn",
    )(*args)


def _out_proj_kernel(x_ref, attn_ref, zs_ref, ya_ref, yc_ref, w_ref, y_ref):
    acc = jnp.dot((attn_ref[...] * zs_ref[...]).astype(BF16), w_ref[0:D_ATTN, :],
                  preferred_element_type=F32)
    acc = acc + jnp.dot(ya_ref[...], w_ref[D_ATTN:D_ATTN + D_CONV, :],
                        preferred_element_type=F32)
    acc = acc + jnp.dot(yc_ref[...], w_ref[D_ATTN + D_CONV:, :], preferred_element_type=F32)
    y_ref[...] = x_ref[...] + acc


def _out_proj(x, attn, zs, ya, yc, w_bf, tm):
    b, s, _ = x.shape
    tile = lambda c: pl.BlockSpec((None, tm, c), lambda bi, i: (bi, i, 0))
    return pl.pallas_call(
        _out_proj_kernel,
        grid=(b, s // tm),
        in_specs=[tile(D_MODEL), tile(D_ATTN), tile(D_ATTN), tile(D_CONV), tile(D_SCONV),
                  pl.BlockSpec(w_bf.shape, lambda bi, i: (0, 0))],
        out_specs=tile(D_MODEL),
        out_shape=jax.ShapeDtypeStruct((b, s, D_MODEL), F32),
        compiler_params=pltpu.CompilerParams(dimension_semantics=("arbitrary", "arbitrary"),
                                             vmem_limit_bytes=VMEM_LIMIT),
        name="out_proj",
    )(x, attn, zs, ya, yc, w_bf)


def _sample_attn_kernel(q_ref, kn_ref, vn_ref, kt_ref, vt_ref, attn_ref):
    l_cache = kt_ref.shape[1]
    kn = kn_ref[...]
    vn = vn_ref[...]
    head_of_lane = lax.broadcasted_iota(jnp.int32, (N_HEADS, D_ATTN), 1) // HEAD_DIM
    own = head_of_lane == lax.broadcasted_iota(jnp.int32, (N_HEADS, D_ATTN), 0)
    qblk = jnp.where(own, q_ref[...], 0.0)
    s = jnp.dot(qblk.astype(BF16), kt_ref[...].astype(BF16), preferred_element_type=F32)
    s_self = jnp.sum(qblk * kn, axis=1, keepdims=True)

    dist = l_cache - lax.broadcasted_iota(jnp.int32, (N_HEADS, l_cache), 1)
    cnt = jnp.zeros((N_HEADS, l_cache), F32)
    for w, d in zip(WINDOWS, DILATIONS):
        cnt = cnt + jnp.where((dist <= w) & ((dist & (d - 1)) == 0), 1.0, 0.0)
    s = jnp.where(cnt > 0, s, NEG)
    m = jnp.maximum(jnp.max(s, axis=1, keepdims=True), s_self)
    e = (cnt * jnp.exp(s - m)).astype(BF16)
    e_self = len(WINDOWS) * jnp.exp(s_self - m)
    den = jnp.sum(e.astype(F32), axis=1, keepdims=True) + e_self
    pv = lax.dot_general(e, vt_ref[...].astype(BF16), (((1,), (1,)), ((), ())),
                         preferred_element_type=F32)
    per_lane = lambda a: jnp.sum(jnp.where(own, a, 0.0), axis=0, keepdims=True)
    num = per_lane(pv) + per_lane(jnp.broadcast_to(e_self, own.shape)) * vn
    attn_ref[...] = num / per_lane(jnp.broadcast_to(den, own.shape))


def _sample_attn(q, kf, vf, cache_kt, cache_vt, layer):
    _, bd, _, l_cache = cache_kt.shape
    assert l_cache >= WINDOWS[-1]
    row = pl.BlockSpec((None, 1, D_ATTN), lambda b: (b, 0, 0))
    cache = pl.BlockSpec((None, None, D_ATTN, l_cache), lambda b: (layer, b, 0, 0))
    r3 = lambda a: a.reshape(bd, 1, D_ATTN)
    attn = pl.pallas_call(
        _sample_attn_kernel,
        grid=(bd,),
        in_specs=[row, row, row, cache, cache],
        out_specs=row,
        out_shape=jax.ShapeDtypeStruct((bd, 1, D_ATTN), F32),
        compiler_params=pltpu.CompilerParams(dimension_semantics=("arbitrary",),
                                             vmem_limit_bytes=VMEM_LIMIT),
        name="sample_attn",
    )(r3(q), r3(kf), r3(vf), cache_kt, cache_vt)
    return attn.reshape(bd, D_ATTN)


def _sample_out_kernel(x_ref, attn_ref, zs_ref, u_ref, gza_ref, bz_ref, uc_ref, sa_ref, sc_ref,
                       aw_ref, ab_ref, ag_ref, alb_ref, cw_ref, w_ref, y_ref, na_ref, nc_ref):
    u = u_ref[...]
    uc = uc_ref[...]
    na = CONV_WIDTH - 1
    nc = SCONV_WIDTH - 1
    ca = ab_ref[...] + aw_ref[na:na + 1, :] * u
    for j in range(na):
        ca = ca + aw_ref[j:j + 1, :] * sa_ref[:, j * D_CONV:(j + 1) * D_CONV]
    ya = _silu(_layernorm_rows(ca, ag_ref[...], alb_ref[...])) * gza_ref[...]
    cc = cw_ref[nc:nc + 1, :] * uc
    for j in range(nc):
        cc = cc + cw_ref[j:j + 1, :] * sc_ref[:, j * D_SCONV:(j + 1) * D_SCONV]
    yc = bz_ref[...] * cc
    acc = jnp.dot((attn_ref[...] * zs_ref[...]).astype(BF16), w_ref[0:D_ATTN, :],
                  preferred_element_type=F32)
    acc = acc + jnp.dot(ya.astype(BF16), w_ref[D_ATTN:D_ATTN + D_CONV, :],
                        preferred_element_type=F32)
    acc = acc + jnp.dot(yc.astype(BF16), w_ref[D_ATTN + D_CONV:, :],
                        preferred_element_type=F32)
    y_ref[...] = x_ref[...] + acc
    na_ref[:, 0:(na - 1) * D_CONV] = sa_ref[:, D_CONV:na * D_CONV]
    na_ref[:, (na - 1) * D_CONV:] = u
    nc_ref[:, 0:(nc - 1) * D_SCONV] = sc_ref[:, D_SCONV:nc * D_SCONV]
    nc_ref[:, (nc - 1) * D_SCONV:] = uc


def _sample_out(x, attn, zs, u, gza, bz, uc, sa, sc, aw, ab, ag, alb, cw, w_bf):
    bd = x.shape[0]
    sa2 = sa.reshape(bd, -1)
    sc2 = sc.reshape(bd, -1)
    args = (x, attn, zs, u, gza, bz, uc, sa2, sc2, aw, ab, ag, alb, cw, w_bf)
    full = lambda a: pl.BlockSpec(a.shape, lambda i: (0,) * a.ndim)
    y, na, nc = pl.pallas_call(
        _sample_out_kernel,
        grid=(1,),
        in_specs=[full(a) for a in args],
        out_specs=(full(x), full(sa2), full(sc2)),
        out_shape=(jax.ShapeDtypeStruct(x.shape, F32),
                   jax.ShapeDtypeStruct(sa2.shape, F32),
                   jax.ShapeDtypeStruct(sc2.shape, F32)),
        compiler_params=pltpu.CompilerParams(dimension_semantics=("arbitrary",),
                                             vmem_limit_bytes=VMEM_LIMIT),
        name="sample_out",
    )(*args)
    return y, na.reshape(sa.shape), nc.reshape(sc.shape)


def _rope_tables(pos):
    half = HEAD_DIM // 2
    inv = ROPE_THETA ** (-jnp.arange(half, dtype=F32) / half)
    ang = pos.astype(F32)[:, None] * inv[None, :]
    cos, sin = jnp.cos(ang), jnp.sin(ang)
    reps = LANES // HEAD_DIM
    cos_t = jnp.tile(jnp.concatenate([cos, cos], axis=-1), (1, reps))
    sin_t = jnp.tile(jnp.concatenate([-sin, sin], axis=-1), (1, reps))
    return cos_t, sin_t


def _segment_mean_matrix():
    seg = np.kron(np.eye(LANES // HEAD_DIM), np.ones((HEAD_DIM, HEAD_DIM))) / HEAD_DIM
    return jnp.asarray(seg, dtype=BF16)


def _to_channel_major(cache):
    lead = cache.shape[:-3]
    n = len(lead)
    perm = tuple(range(n)) + (n + 1, n + 2, n)
    return jnp.transpose(cache, perm).reshape(*lead, D_ATTN, cache.shape[-3])


def _from_channel_major(cache_t):
    lead = cache_t.shape[:-2]
    n = len(lead)
    perm = tuple(range(n)) + (n + 2, n, n + 1)
    return jnp.transpose(cache_t.reshape(*lead, N_HEADS, HEAD_DIM, cache_t.shape[-1]), perm)


def kernel(x_prompt, x_sample, cache_k, cache_v, state_conv_a, state_conv_c, ln_g, w_in,
           q_norm_g, k_norm_g, a_conv_w, a_conv_b, a_ln_g, a_ln_b, c_conv_w, w_out):
    bp, sp, _ = x_prompt.shape
    bd, ts, _ = x_sample.shape
    assert ts == 1
    depth = w_in.shape[0]
    l_prompt = min(WINDOWS[-1], sp)

    cos_p, sin_p = _rope_tables(jnp.arange(sp, dtype=jnp.int32))
    cos_s, sin_s = _rope_tables(jnp.full((bd,), PAST_LEN, dtype=jnp.int32))
    seg = _segment_mean_matrix()
    row = lambda a: a.reshape(1, -1)
    tile_g = lambda g: jnp.tile(g, LANES // HEAD_DIM).reshape(1, LANES)
    cache_kt = _to_channel_major(cache_k)
    cache_vt = _to_channel_major(cache_v)

    w_in_bf = w_in.astype(BF16)
    w_out_bf = w_out.astype(BF16)
    layer_w = []
    for l in range(depth):
        conv_w = (a_conv_w[l], row(a_conv_b[l]), row(a_ln_g[l]), row(a_ln_b[l]), c_conv_w[l])
        layer_w.append((row(ln_g[l]), tile_g(q_norm_g[l]), tile_g(k_norm_g[l]), conv_w))

    hs = x_sample.reshape(bd, D_MODEL)
    sk, sv, sa, sc = [], [], [], []
    for l, (lng, qg, kg, conv_w) in enumerate(layer_w):
        q, kf, vf, zs, u, gza, bz, uc = _in_proj_sample(hs, lng, w_in_bf[l], cos_s, sin_s, qg, kg,
                                                        seg)
        attn = _sample_attn(q, kf, vf, cache_kt, cache_vt, l)
        hs, na, nc = _sample_out(hs, attn, zs, u, gza, bz, uc, state_conv_a[l], state_conv_c[l],
                                 *conv_w, w_out_bf[l])
        sk.append(kf)
        sv.append(vf)
        sa.append(na)
        sc.append(nc)
    new_cols = lambda rows: jnp.transpose(jnp.stack(rows), (0, 2, 1))


    assert depth >= 2
    shift_jobs = {depth - 2: (cache_kt, new_cols(sk)), depth - 1: (cache_vt, new_cols(sv))}
    shifted = {}

    hp = x_prompt
    pk, pv, pa, pc = [], [], [], []
    for l, (lng, qg, kg, conv_w) in enumerate(layer_w):
        aw8 = jnp.broadcast_to(a_conv_w[l][:, None, :], (CONV_WIDTH, SUBLANES, D_CONV))
        outs = _in_proj_prompt(hp, lng, w_in_bf[l], cos_p, sin_p, qg, kg, seg,
                               (aw8,) + conv_w[1:], PROMPT_TM_IN, l_prompt,
                               shift=shift_jobs.get(l))
        if l in shift_jobs:
            outs, shifted[l] = outs[:-1], outs[-1]
        n_pat = len(DILATIONS)
        q_d, k_d, v_d = outs[0:n_pat], outs[n_pat:2 * n_pat], outs[2 * n_pat:3 * n_pat]
        kt, vt, zs, ya, yc, u_last, uc_last = outs[3 * n_pat:]
        attn = _attention(list(zip(q_d, k_d, v_d)), bp, sp)
        hp = _out_proj(hp, attn, zs, ya, yc, w_out_bf[l], PROMPT_TM_OUT)
        pk.append(kt)
        pv.append(vt)
        pa.append(u_last[:, A_HALO - (CONV_WIDTH - 1):])
        pc.append(uc_last[:, C_HALO - (SCONV_WIDTH - 1):])

    return (hp, hs.reshape(bd, ts, D_MODEL),
            _from_channel_major(jnp.stack(pk)), _from_channel_major(jnp.stack(pv)),
            jnp.stack(pa), jnp.stack(pc),
            _from_channel_major(shifted[depth - 2]), _from_channel_major(shifted[depth - 1]),
            jnp.stack(sa), jnp.stack(sc))
```

```python
import functools

import numpy as np
import jax
import jax.numpy as jnp
from jax import lax
from jax.experimental import pallas as pl
from jax.experimental.pallas import tpu as pltpu

D_MODEL = 1024
N_HEADS = 8
HEAD_DIM = 64
D_ATTN = N_HEADS * HEAD_DIM
D_CONV = 256
CONV_WIDTH = 31
D_SCONV = 256
SCONV_WIDTH = 3
WINDOWS = (128, 512, 2048)
DILATIONS = (1, 4, 16)
WIN_KEYS = 128
assert all(w // d == WIN_KEYS for w, d in zip(WINDOWS, DILATIONS))
assert all(d & (d - 1) == 0 for d in DILATIONS)
assert len(DILATIONS) == 3 and DILATIONS[0] == 1 and DILATIONS[2] % DILATIONS[1] == 0
PAST_LEN = 16384
ROPE_THETA = 10000.0
EPS = 1e-6
NEG = -1e30
LOG2_E = 1.4426950408889634

C_Q, C_K, C_V, C_Z = 0, 512, 1024, 1536
C_AVAL, C_AGATE, C_ZA = 2048, 2304, 2560
C_CB, C_CC, C_CH, C_ZC = 2816, 3072, 3328, 3584
D_IN = 3840

LANES = 128
SUBLANES = 8
N_SLABS = D_ATTN // LANES
VMEM_LIMIT = 56 * 1024 * 1024

PROMPT_TM_IN = 512
PROMPT_TM_OUT = 512
KT_DILATIONS = tuple(d for d in DILATIONS if (PROMPT_TM_IN // d) % LANES == 0)

F32 = jnp.float32
BF16 = jnp.bfloat16


def _sigmoid(x):
    return 1.0 / (1.0 + jnp.exp(-x))


def _silu(x):
    return x * _sigmoid(x)


SHIFT_ROWS = 32


def _shift_cache_block(c_ref, nt_ref, n_ref, b):
    n_rows, l_cache = c_ref.shape
    bd = nt_ref.shape[1]
    mine = lax.broadcasted_iota(jnp.int32, (SHIFT_ROWS, bd), 1) == b
    last_lane = lax.broadcasted_iota(jnp.int32, (SHIFT_ROWS, LANES), 1) == LANES - 1
    for c in range(n_rows // SHIFT_ROWS):
        rows = slice(c * SHIFT_ROWS, (c + 1) * SHIFT_ROWS)
        new_col = jnp.sum(jnp.where(mine, nt_ref[rows, :], 0.0), axis=1, keepdims=True)
        rolled = pltpu.roll(c_ref[rows, :], l_cache - 1, 1)
        n_ref[rows, 0:l_cache - LANES] = rolled[:, 0:l_cache - LANES]
        n_ref[rows, l_cache - LANES:] = jnp.where(last_lane, new_col, rolled[:, l_cache - LANES:])


def _shift_specs(depth, bd, l_cache, step_of):
    cache = pl.BlockSpec((None, None, D_ATTN, l_cache),
                         lambda *g: (step_of(*g) // bd, step_of(*g) % bd, 0, 0))
    cols = pl.BlockSpec((None, D_ATTN, bd), lambda *g: (step_of(*g) // bd, 0, 0))
    return cache, cols


def _in_proj_body(x_ref, lng_ref, w_ref, cos_ref, sin_ref, qg_ref, kg_ref, seg_ref,
                  zs_ref, u_ref, gza_ref, bz_ref, uc_ref, emit_q, emit_k, emit_v, q_scale):
    x = x_ref[...]
    ms = jnp.mean(x * x, axis=-1, keepdims=True)
    xn = (x * lax.rsqrt(ms + EPS) * lng_ref[...]).astype(BF16)

    def proj(c0, c1):
        return jnp.dot(xn, w_ref[:, c0:c1], preferred_element_type=F32)

    cos = cos_ref[...]
    sin = sin_ref[...]
    tm = x.shape[0]
    first_half = (lax.broadcasted_iota(jnp.int32, (tm, LANES), 1) & (HEAD_DIM // 2)) == 0

    def norm_rope(p, g_ref, scale, emit):
        for s in range(N_SLABS):
            ps = p[:, s * LANES:(s + 1) * LANES]
            ss = jnp.dot((ps * ps).astype(BF16), seg_ref[...], preferred_element_type=F32)
            pn = ps * lax.rsqrt(ss + EPS) * g_ref[...]
            partner = jnp.where(first_half,
                                pltpu.roll(pn, LANES - HEAD_DIM // 2, 1),
                                pltpu.roll(pn, HEAD_DIM // 2, 1))
            emit(s, (pn * cos + partner * sin) * scale)

    norm_rope(proj(C_Q, C_K), qg_ref, q_scale, emit_q)
    norm_rope(proj(C_K, C_V), kg_ref, 1.0, emit_k)
    v = proj(C_V, C_Z)
    for s in range(N_SLABS):
        emit_v(s, v[:, s * LANES:(s + 1) * LANES])
    zs_ref[...] = _silu(proj(C_Z, C_AVAL)).astype(zs_ref.dtype)
    ag = proj(C_AVAL, C_ZA)
    u_ref[...] = ag[:, :D_CONV] * _sigmoid(ag[:, D_CONV:])
    zb = proj(C_ZA, C_CC)
    gza_ref[...] = _silu(zb[:, :D_CONV]).astype(gza_ref.dtype)
    zc = proj(C_ZC, D_IN)
    bz_ref[...] = (zb[:, D_CONV:] * _silu(zc)).astype(bz_ref.dtype)
    ch = proj(C_CC, C_ZC)
    uc_ref[...] = ch[:, :D_SCONV] * ch[:, D_SCONV:]


def _in_proj_prompt_kernel(x_ref, lng_ref, w_ref, cos_ref, sin_ref, qg_ref, kg_ref, seg_ref,
                           *rest, tm, shift):
    if shift:
        (c_ref, nt_ref, q1_ref, q4_ref, q16_ref, k1_ref, k4_ref, k16_ref, v1_ref, v4_ref, v16_ref,
         kt_ref, vt_ref, zs_ref, u_ref, gza_ref, bz_ref, uc_ref, n_ref,
         stage_ref, stage_next_ref) = rest
        grid_step = pl.program_id(0) * pl.num_programs(1) + pl.program_id(1)
        _shift_cache_block(c_ref, nt_ref, n_ref, grid_step % nt_ref.shape[1])
    else:
        (q1_ref, q4_ref, q16_ref, k1_ref, k4_ref, k16_ref, v1_ref, v4_ref, v16_ref,
         kt_ref, vt_ref, zs_ref, u_ref, gza_ref, bz_ref, uc_ref,
         stage_ref, stage_next_ref) = rest

    def emitter(which, outs, t_ref, transposed=()):
        def emit(s, val):
            slot = (which * N_SLABS + s) % 2
            levels = {1: stage_ref.at[slot]}
            levels[1][0] = val
            val_t = val.T if t_ref is not None else None
            for li, (d, out) in enumerate(zip(DILATIONS, outs)):
                prev = DILATIONS[li - 1] if li else 1
                keep = li + 1 < len(DILATIONS)
                for r in range(d):
                    if d == 1:
                        part = val
                    else:
                        part = levels[prev][r % prev, pl.ds(r // prev, tm // d, stride=d // prev), :]
                        if keep:
                            stage_next_ref[slot, r] = part
                    if d in transposed:
                        part = val_t if d == 1 else part.T
                    out[s, r] = part.astype(BF16)
                if d != 1 and keep:
                    levels[d] = stage_next_ref.at[slot]
            if t_ref is not None:
                t_ref[s * LANES:(s + 1) * LANES, :] = val_t
        return emit

    _in_proj_body(x_ref, lng_ref, w_ref, cos_ref, sin_ref, qg_ref, kg_ref, seg_ref,
                  zs_ref, u_ref, gza_ref, bz_ref, uc_ref,
                  emitter(0, (q1_ref, q4_ref, q16_ref), None),
                  emitter(1, (k1_ref, k4_ref, k16_ref), kt_ref, transposed=KT_DILATIONS),
                  emitter(2, (v1_ref, v4_ref, v16_ref), vt_ref),
                  q_scale=HEAD_DIM ** -0.5 * LOG2_E)


def _in_proj_prompt(x, lng, w_bf, cos_t, sin_t, qg, kg, seg, tm, l_tail, shift=None):
    b, s, _ = x.shape
    grid = (b, s // tm)
    n_skip = (s - l_tail) // tm
    tile = lambda c: pl.BlockSpec((None, tm, c), lambda bi, i: (bi, i, 0))
    full = lambda a: pl.BlockSpec(a.shape, lambda bi, i: (0,) * a.ndim)
    pos = pl.BlockSpec((tm, LANES), lambda bi, i: (i, 0))
    def deint(d, transposed):
        blk = (LANES, tm // d) if transposed else (tm // d, LANES)
        idx = (lambda bi, i: (bi, 0, 0, 0, i)) if transposed else (lambda bi, i: (bi, 0, 0, i, 0))
        dims = (LANES, s // d) if transposed else (s // d, LANES)
        return (pl.BlockSpec((None, N_SLABS, d) + blk, idx),
                jax.ShapeDtypeStruct((b, N_SLABS, d) + dims, BF16))

    qkv = [deint(d, which == 1 and d in KT_DILATIONS) for which in range(3) for d in DILATIONS]
    tail = pl.BlockSpec((None, D_ATTN, tm), lambda bi, i: (bi, 0, jnp.maximum(i - n_skip, 0)))
    out_shapes = tuple(shape for _, shape in qkv) + (
        jax.ShapeDtypeStruct((b, D_ATTN, l_tail), F32),
        jax.ShapeDtypeStruct((b, D_ATTN, l_tail), F32),
        jax.ShapeDtypeStruct((b, s, D_ATTN), BF16),
        jax.ShapeDtypeStruct((b, s, D_CONV), F32),
        jax.ShapeDtypeStruct((b, s, D_CONV), BF16),
        jax.ShapeDtypeStruct((b, s, D_SCONV), BF16),
        jax.ShapeDtypeStruct((b, s, D_SCONV), F32),
    )
    out_specs = tuple(spec for spec, _ in qkv) + (
        tail, tail, tile(D_ATTN), tile(D_CONV), tile(D_CONV), tile(D_SCONV), tile(D_SCONV))
    w_spec = pl.BlockSpec(w_bf.shape, lambda bi, i: (0, 0), pipeline_mode=pl.Buffered(1))
    in_specs = [tile(D_MODEL), full(lng), w_spec, pos, pos, full(qg), full(kg), full(seg)]
    args = [x, lng, w_bf, cos_t, sin_t, qg, kg, seg]
    if shift is not None:
        cache, cols = shift
        depth, bd, _, l_cache = cache.shape
        assert depth * bd == grid[0] * grid[1]
        cache_spec, cols_spec = _shift_specs(depth, bd, l_cache, lambda bi, i: bi * grid[1] + i)
        in_specs += [cache_spec, cols_spec]
        args += [cache, cols]
        out_specs += (cache_spec,)
        out_shapes += (jax.ShapeDtypeStruct(cache.shape, F32),)
    return pl.pallas_call(
        functools.partial(_in_proj_prompt_kernel, tm=tm, shift=shift is not None),
        grid=grid,
        in_specs=in_specs,
        out_specs=out_specs,
        out_shape=out_shapes,
        scratch_shapes=[pltpu.VMEM((2, 1, tm, LANES), F32),
                        pltpu.VMEM((2, DILATIONS[1], tm // DILATIONS[1], LANES), F32)],
        compiler_params=pltpu.CompilerParams(dimension_semantics=("arbitrary", "arbitrary"),
                                             vmem_limit_bytes=VMEM_LIMIT),
        name="in_proj",
    )(*args)


def _in_proj_sample_kernel(x_ref, lng_ref, w_ref, cos_ref, sin_ref, qg_ref, kg_ref, seg_ref,
                           q_ref, k_ref, v_ref, zs_ref, u_ref, gza_ref, bz_ref, uc_ref):
    def emitter(out):
        def emit(s, val):
            out[:, s * LANES:(s + 1) * LANES] = val
        return emit

    _in_proj_body(x_ref, lng_ref, w_ref, cos_ref, sin_ref, qg_ref, kg_ref, seg_ref,
                  zs_ref, u_ref, gza_ref, bz_ref, uc_ref,
                  emitter(q_ref), emitter(k_ref), emitter(v_ref), q_scale=HEAD_DIM ** -0.5)


def _in_proj_sample(x2d, lng, w_bf, cos_t, sin_t, qg, kg, seg):
    n = x2d.shape[0]
    args = (x2d, lng, w_bf, cos_t, sin_t, qg, kg, seg)
    full = lambda a: pl.BlockSpec(a.shape, lambda i: (0,) * a.ndim)
    widths = (D_ATTN, D_ATTN, D_ATTN, D_ATTN, D_CONV, D_CONV, D_SCONV, D_SCONV)
    out_shapes = tuple(jax.ShapeDtypeStruct((n, c), F32) for c in widths)
    return pl.pallas_call(
        _in_proj_sample_kernel,
        grid=(1,),
        in_specs=[full(a) for a in args],
        out_specs=tuple(full(s) for s in out_shapes),
        out_shape=out_shapes,
        compiler_params=pltpu.CompilerParams(dimension_semantics=("arbitrary",),
                                             vmem_limit_bytes=VMEM_LIMIT),
        name="in_proj_sample",
    )(*args)


ATTN_TILE = 2048
assert all(ATTN_TILE % (WIN_KEYS * d) == 0 for d in DILATIONS)
ATTN_UNROLL = 8
assert (ATTN_TILE // WIN_KEYS) % ATTN_UNROLL == 0


def _attn_kernel(*refs, tile):
    n_pat = len(DILATIONS)
    qkv_refs = [refs[3 * p:3 * p + 3] for p in range(n_pat)]
    bias_ref, o_ref, acc_ref, m_ref, l_ref = refs[3 * n_pat:]
    t = pl.program_id(2)
    blk = WIN_KEYS
    n_blocks = tile // blk
    lane = lax.broadcasted_iota(jnp.int32, (1, LANES), 1)
    head_a = lane < HEAD_DIM

    def block_softmax(qb, k_ref, v_ref, r, q0, k_channel_major):
        kk = pl.multiple_of(jnp.maximum(q0 - blk, 0), blk)
        bias = bias_ref[jnp.minimum(q0, 1)]
        if k_channel_major:
            kb = k_ref[r, :, pl.ds(kk, 2 * blk)]
            contract = (((1,), (0,)), ((), ()))
        else:
            kb = k_ref[r, pl.ds(kk, 2 * blk), :]
            contract = (((1,), (1,)), ((), ()))
        vb = v_ref[r, pl.ds(kk, 2 * blk), :]
        res, mx = [], []
        for sel in (head_a, jnp.logical_not(head_a)):
            qh = jnp.where(sel, qb, jnp.zeros_like(qb))
            s = lax.dot_general(qh, kb, contract, preferred_element_type=F32)
            s = s + bias
            m = jnp.max(s, axis=1, keepdims=True)
            p = jnp.exp2(s - m).astype(BF16)
            vh = jnp.where(sel, vb, jnp.ones_like(vb))
            res.append(jnp.dot(p, vh, preferred_element_type=F32))
            mx.append(m)
        num = jnp.where(head_a, res[0], res[1])
        den = pltpu.roll(jnp.where(head_a, res[1], res[0]), HEAD_DIM, 1)
        return num, den, jnp.where(head_a, mx[0], mx[1])

    order = sorted(range(n_pat), key=lambda p: -DILATIONS[p])
    for step, p in enumerate(order):
        d = DILATIONS[p]
        q_ref, k_ref, v_ref = qkv_refs[p]
        per_res = n_blocks // d
        assert per_res & (per_res - 1) == 0
        first, last = step == 0, step == n_pat - 1

        def body(it, carry, d=d, q_ref=q_ref, k_ref=k_ref, v_ref=v_ref, per_res=per_res,
                 first=first, last=last):
            for u in range(ATTN_UNROLL):
                i = it * ATTN_UNROLL + u
                r = lax.shift_right_logical(i, per_res.bit_length() - 1)
                jb = i & (per_res - 1)
                qb = q_ref[r, pl.ds(pl.multiple_of(jb * blk, blk), blk), :]
                num, den, mb = block_softmax(qb, k_ref, v_ref, r, t * (tile // d) + jb * blk,
                                             d in KT_DILATIONS)
                start = jb * blk * d + r
                rows = pl.ds(start, blk) if d == 1 else pl.ds(start, blk, stride=d)
                if not first:
                    m_old = m_ref[rows, :]
                    m_new = jnp.maximum(m_old, mb)
                    w_old = jnp.exp2(m_old - m_new)
                    w_blk = jnp.exp2(mb - m_new)
                    num = acc_ref[rows, :] * w_old + num * w_blk
                    den = l_ref[rows, :] * w_old + den * w_blk
                    mb = m_new
                if last:
                    o_ref[rows, :] = num / den
                else:
                    acc_ref[rows, :] = num
                    l_ref[rows, :] = den
                    m_ref[rows, :] = mb
            return carry

        lax.fori_loop(0, n_blocks // ATTN_UNROLL, body, 0)


def _band_bias():
    rel = np.arange(2 * WIN_KEYS)[None, :] - np.arange(WIN_KEYS)[:, None]
    variants = [rel - off for off in (0, WIN_KEYS)]
    return jnp.asarray(np.stack([np.where((v <= 0) & (v >= -WIN_KEYS), 0.0, NEG)
                                 for v in variants]), dtype=F32)


def _attention(qkv, b, s):
    tile = ATTN_TILE
    in_specs, args = [], []
    for d, (q, k, v) in zip(DILATIONS, qkv):
        qspec = pl.BlockSpec((None, None, d, tile // d, LANES), lambda bi, c, i: (bi, c, 0, i, 0))
        kvspec = pl.BlockSpec((None, None, d, s // d, LANES), lambda bi, c, i: (bi, c, 0, 0, 0))
        ktspec = pl.BlockSpec((None, None, d, LANES, s // d), lambda bi, c, i: (bi, c, 0, 0, 0))
        in_specs += [qspec, ktspec if d in KT_DILATIONS else kvspec, kvspec]
        args += [q, k, v]
    bias = _band_bias()
    in_specs.append(pl.BlockSpec(bias.shape, lambda bi, c, i: (0, 0, 0)))
    args.append(bias)
    return pl.pallas_call(
        functools.partial(_attn_kernel, tile=tile),
        grid=(b, N_SLABS, s // tile),
        in_specs=in_specs,
        out_specs=pl.BlockSpec((None, tile, LANES), lambda bi, c, i: (bi, i, c)),
        out_shape=jax.ShapeDtypeStruct((b, s, D_ATTN), F32),
        scratch_shapes=[pltpu.VMEM((tile, LANES), F32)] * 3,
        compiler_params=pltpu.CompilerParams(
            dimension_semantics=("arbitrary", "arbitrary", "arbitrary"),
            vmem_limit_bytes=VMEM_LIMIT),
        name="dilated_attn",
    )(*args)


A_HALO = 32
C_HALO = 8
OUT_CHUNK = 128


def _layernorm_rows(x, g, b):
    mu = jnp.mean(x, axis=-1, keepdims=True)
    xc = x - mu
    var = jnp.mean(xc * xc, axis=-1, keepdims=True)
    return xc * lax.rsqrt(var + EPS) * g + b


def _out_proj_kernel(x_ref, attn_ref, zs_ref, u_ref, uh_ref, gza_ref, bz_ref, uc_ref, uch_ref,
                     aw_ref, ab_ref, ag_ref, alb_ref, cw_ref, w_ref,
                     y_ref, ubuf, uphase, ucbuf, *, tm):
    first_tile = pl.program_id(1) == 0
    y_ref[...] = x_ref[...] + jnp.dot((attn_ref[...] * zs_ref[...]).astype(BF16),
                                      w_ref[0:D_ATTN, :], preferred_element_type=F32)

    ubuf[0:A_HALO, :] = jnp.where(first_tile, 0.0, uh_ref[...])
    ubuf[A_HALO:A_HALO + tm, :] = u_ref[...]
    n_rows = tm + A_HALO - SUBLANES
    for b in range(1, SUBLANES):
        uphase[b - 1] = ubuf[b:b + n_rows, :]
    ucbuf[0:C_HALO, :] = jnp.where(first_tile, 0.0, uch_ref[...])
    ucbuf[C_HALO:C_HALO + tm, :] = uc_ref[...]

    base = A_HALO - (CONV_WIDTH - 1)
    cbase = C_HALO - (SCONV_WIDTH - 1)
    groups = (OUT_CHUNK // SUBLANES, SUBLANES, D_CONV)
    for c in range(tm // OUT_CHUNK):
        r0 = c * OUT_CHUNK
        rows = slice(r0, r0 + OUT_CHUNK)
        ca = jnp.zeros(groups, F32)
        for j in range(CONV_WIDTH):
            b = (base + j) % SUBLANES
            a8 = base + j - b + r0
            src = (ubuf[a8:a8 + OUT_CHUNK, :] if b == 0
                   else uphase[b - 1, a8:a8 + OUT_CHUNK, :])
            ca = ca + aw_ref[j][None] * src.reshape(groups)
        ca = ca.reshape(OUT_CHUNK, D_CONV) + ab_ref[...]
        ya = _silu(_layernorm_rows(ca, ag_ref[...], alb_ref[...])) * gza_ref[rows, :]
        cc = jnp.zeros((OUT_CHUNK, D_SCONV), F32)
        for j in range(SCONV_WIDTH):
            cc = cc + cw_ref[j:j + 1, :] * ucbuf[cbase + j + r0:cbase + j + r0 + OUT_CHUNK, :]
        yc = bz_ref[rows, :] * cc
        y_ref[rows, :] += (
            jnp.dot(ya.astype(BF16), w_ref[D_ATTN:D_ATTN + D_CONV, :], preferred_element_type=F32)
            + jnp.dot(yc.astype(BF16), w_ref[D_ATTN + D_CONV:, :], preferred_element_type=F32))


def _out_proj(x, attn, zs, u, gza, bz, uc, aw, ab, ag, alb, cw, w_bf, tm):
    b, s, _ = x.shape
    tile = lambda c: pl.BlockSpec((None, tm, c), lambda bi, i: (bi, i, 0))
    halo = lambda h, c: pl.BlockSpec(
        (None, h, c), lambda bi, i: (bi, jnp.maximum(i * (tm // h) - 1, 0), 0))
    full = lambda a: pl.BlockSpec(a.shape, lambda bi, i: (0,) * a.ndim)
    return pl.pallas_call(
        functools.partial(_out_proj_kernel, tm=tm),
        grid=(b, s // tm),
        in_specs=[tile(D_MODEL), tile(D_ATTN), tile(D_ATTN),
                  tile(D_CONV), halo(A_HALO, D_CONV), tile(D_CONV), tile(D_SCONV),
                  tile(D_SCONV), halo(C_HALO, D_SCONV)]
                 + [full(aw), full(ab), full(ag), full(alb), full(cw), full(w_bf)],
        out_specs=tile(D_MODEL),
        out_shape=jax.ShapeDtypeStruct((b, s, D_MODEL), F32),
        scratch_shapes=[pltpu.VMEM((A_HALO + tm, D_CONV), F32),
                        pltpu.VMEM((SUBLANES - 1, A_HALO + tm - SUBLANES, D_CONV), F32),
                        pltpu.VMEM((C_HALO + tm, D_SCONV), F32)],
        compiler_params=pltpu.CompilerParams(dimension_semantics=("arbitrary", "arbitrary"),
                                             vmem_limit_bytes=VMEM_LIMIT),
        name="out_proj",
    )(x, attn, zs, u, u, gza, bz, uc, uc, aw, ab, ag, alb, cw, w_bf)


def _sample_attn_kernel(q_ref, kn_ref, vn_ref, kt_ref, vt_ref, attn_ref):
    l_cache = kt_ref.shape[1]
    kn = kn_ref[...]
    vn = vn_ref[...]
    head_of_lane = lax.broadcasted_iota(jnp.int32, (N_HEADS, D_ATTN), 1) // HEAD_DIM
    own = head_of_lane == lax.broadcasted_iota(jnp.int32, (N_HEADS, D_ATTN), 0)
    qblk = jnp.where(own, q_ref[...], 0.0)
    s = jnp.dot(qblk.astype(BF16), kt_ref[...].astype(BF16), preferred_element_type=F32)
    s_self = jnp.sum(qblk * kn, axis=1, keepdims=True)

    dist = l_cache - lax.broadcasted_iota(jnp.int32, (N_HEADS, l_cache), 1)
    cnt = jnp.zeros((N_HEADS, l_cache), F32)
    for w, d in zip(WINDOWS, DILATIONS):
        cnt = cnt + jnp.where((dist <= w) & ((dist & (d - 1)) == 0), 1.0, 0.0)
    s = jnp.where(cnt > 0, s, NEG)
    m = jnp.maximum(jnp.max(s, axis=1, keepdims=True), s_self)
    e = (cnt * jnp.exp(s - m)).astype(BF16)
    e_self = len(WINDOWS) * jnp.exp(s_self - m)
    den = jnp.sum(e.astype(F32), axis=1, keepdims=True) + e_self
    pv = lax.dot_general(e, vt_ref[...].astype(BF16), (((1,), (1,)), ((), ())),
                         preferred_element_type=F32)
    per_lane = lambda a: jnp.sum(jnp.where(own, a, 0.0), axis=0, keepdims=True)
    num = per_lane(pv) + per_lane(jnp.broadcast_to(e_self, own.shape)) * vn
    attn_ref[...] = num / per_lane(jnp.broadcast_to(den, own.shape))


def _sample_attn(q, kf, vf, cache_kt, cache_vt, layer):
    _, bd, _, l_cache = cache_kt.shape
    assert l_cache >= WINDOWS[-1]
    row = pl.BlockSpec((None, 1, D_ATTN), lambda b: (b, 0, 0))
    cache = pl.BlockSpec((None, None, D_ATTN, l_cache), lambda b: (layer, b, 0, 0))
    r3 = lambda a: a.reshape(bd, 1, D_ATTN)
    attn = pl.pallas_call(
        _sample_attn_kernel,
        grid=(bd,),
        in_specs=[row, row, row, cache, cache],
        out_specs=row,
        out_shape=jax.ShapeDtypeStruct((bd, 1, D_ATTN), F32),
        compiler_params=pltpu.CompilerParams(dimension_semantics=("arbitrary",),
                                             vmem_limit_bytes=VMEM_LIMIT),
        name="sample_attn",
    )(r3(q), r3(kf), r3(vf), cache_kt, cache_vt)
    return attn.reshape(bd, D_ATTN)


def _sample_out_kernel(x_ref, attn_ref, zs_ref, u_ref, gza_ref, bz_ref, uc_ref, sa_ref, sc_ref,
                       aw_ref, ab_ref, ag_ref, alb_ref, cw_ref, w_ref, y_ref, na_ref, nc_ref):
    u = u_ref[...]
    uc = uc_ref[...]
    na = CONV_WIDTH - 1
    nc = SCONV_WIDTH - 1
    ca = ab_ref[...] + aw_ref[na:na + 1, :] * u
    for j in range(na):
        ca = ca + aw_ref[j:j + 1, :] * sa_ref[:, j * D_CONV:(j + 1) * D_CONV]
    ya = _silu(_layernorm_rows(ca, ag_ref[...], alb_ref[...])) * gza_ref[...]
    cc = cw_ref[nc:nc + 1, :] * uc
    for j in range(nc):
        cc = cc + cw_ref[j:j + 1, :] * sc_ref[:, j * D_SCONV:(j + 1) * D_SCONV]
    yc = bz_ref[...] * cc
    acc = jnp.dot((attn_ref[...] * zs_ref[...]).astype(BF16), w_ref[0:D_ATTN, :],
                  preferred_element_type=F32)
    acc = acc + jnp.dot(ya.astype(BF16), w_ref[D_ATTN:D_ATTN + D_CONV, :],
                        preferred_element_type=F32)
    acc = acc + jnp.dot(yc.astype(BF16), w_ref[D_ATTN + D_CONV:, :],
                        preferred_element_type=F32)
    y_ref[...] = x_ref[...] + acc
    na_ref[:, 0:(na - 1) * D_CONV] = sa_ref[:, D_CONV:na * D_CONV]
    na_ref[:, (na - 1) * D_CONV:] = u
    nc_ref[:, 0:(nc - 1) * D_SCONV] = sc_ref[:, D_SCONV:nc * D_SCONV]
    nc_ref[:, (nc - 1) * D_SCONV:] = uc


def _sample_out(x, attn, zs, u, gza, bz, uc, sa, sc, aw, ab, ag, alb, cw, w_bf):
    bd = x.shape[0]
    sa2 = sa.reshape(bd, -1)
    sc2 = sc.reshape(bd, -1)
    args = (x, attn, zs, u, gza, bz, uc, sa2, sc2, aw, ab, ag, alb, cw, w_bf)
    full = lambda a: pl.BlockSpec(a.shape, lambda i: (0,) * a.ndim)
    y, na, nc = pl.pallas_call(
        _sample_out_kernel,
        grid=(1,),
        in_specs=[full(a) for a in args],
        out_specs=(full(x), full(sa2), full(sc2)),
        out_shape=(jax.ShapeDtypeStruct(x.shape, F32),
                   jax.ShapeDtypeStruct(sa2.shape, F32),
                   jax.ShapeDtypeStruct(sc2.shape, F32)),
        compiler_params=pltpu.CompilerParams(dimension_semantics=("arbitrary",),
                                             vmem_limit_bytes=VMEM_LIMIT),
        name="sample_out",
    )(*args)
    return y, na.reshape(sa.shape), nc.reshape(sc.shape)


def _rope_tables(pos):
    half = HEAD_DIM // 2
    inv = ROPE_THETA ** (-jnp.arange(half, dtype=F32) / half)
    ang = pos.astype(F32)[:, None] * inv[None, :]
    cos, sin = jnp.cos(ang), jnp.sin(ang)
    reps = LANES // HEAD_DIM
    cos_t = jnp.tile(jnp.concatenate([cos, cos], axis=-1), (1, reps))
    sin_t = jnp.tile(jnp.concatenate([-sin, sin], axis=-1), (1, reps))
    return cos_t, sin_t


def _segment_mean_matrix():
    seg = np.kron(np.eye(LANES // HEAD_DIM), np.ones((HEAD_DIM, HEAD_DIM))) / HEAD_DIM
    return jnp.asarray(seg, dtype=BF16)


def _to_channel_major(cache):
    lead = cache.shape[:-3]
    n = len(lead)
    perm = tuple(range(n)) + (n + 1, n + 2, n)
    return jnp.transpose(cache, perm).reshape(*lead, D_ATTN, cache.shape[-3])


def _from_channel_major(cache_t):
    lead = cache_t.shape[:-2]
    n = len(lead)
    perm = tuple(range(n)) + (n + 2, n, n + 1)
    return jnp.transpose(cache_t.reshape(*lead, N_HEADS, HEAD_DIM, cache_t.shape[-1]), perm)


def kernel(x_prompt, x_sample, cache_k, cache_v, state_conv_a, state_conv_c, ln_g, w_in,
           q_norm_g, k_norm_g, a_conv_w, a_conv_b, a_ln_g, a_ln_b, c_conv_w, w_out):
    bp, sp, _ = x_prompt.shape
    bd, ts, _ = x_sample.shape
    assert ts == 1
    depth = w_in.shape[0]
    l_prompt = min(WINDOWS[-1], sp)

    cos_p, sin_p = _rope_tables(jnp.arange(sp, dtype=jnp.int32))
    cos_s, sin_s = _rope_tables(jnp.full((bd,), PAST_LEN, dtype=jnp.int32))
    seg = _segment_mean_matrix()
    row = lambda a: a.reshape(1, -1)
    tile_g = lambda g: jnp.tile(g, LANES // HEAD_DIM).reshape(1, LANES)
    cache_kt = _to_channel_major(cache_k)
    cache_vt = _to_channel_major(cache_v)

    w_in_bf = w_in.astype(BF16)
    w_out_bf = w_out.astype(BF16)
    layer_w = []
    for l in range(depth):
        conv_w = (a_conv_w[l], row(a_conv_b[l]), row(a_ln_g[l]), row(a_ln_b[l]), c_conv_w[l])
        layer_w.append((row(ln_g[l]), tile_g(q_norm_g[l]), tile_g(k_norm_g[l]), conv_w))

    hs = x_sample.reshape(bd, D_MODEL)
    sk, sv, sa, sc = [], [], [], []
    for l, (lng, qg, kg, conv_w) in enumerate(layer_w):
        q, kf, vf, zs, u, gza, bz, uc = _in_proj_sample(hs, lng, w_in_bf[l], cos_s, sin_s, qg, kg,
                                                        seg)
        attn = _sample_attn(q, kf, vf, cache_kt, cache_vt, l)
        hs, na, nc = _sample_out(hs, attn, zs, u, gza, bz, uc, state_conv_a[l], state_conv_c[l],
                                 *conv_w, w_out_bf[l])
        sk.append(kf)
        sv.append(vf)
        sa.append(na)
        sc.append(nc)
    new_cols = lambda rows: jnp.transpose(jnp.stack(rows), (0, 2, 1))


    assert depth >= 2
    shift_jobs = {depth - 2: (cache_kt, new_cols(sk)), depth - 1: (cache_vt, new_cols(sv))}
    shifted = {}

    hp = x_prompt
    pk, pv, pa, pc = [], [], [], []
    for l, (lng, qg, kg, conv_w) in enumerate(layer_w):
        outs = _in_proj_prompt(hp, lng, w_in_bf[l], cos_p, sin_p, qg, kg, seg, PROMPT_TM_IN,
                               l_prompt, shift=shift_jobs.get(l))
        if l in shift_jobs:
            outs, shifted[l] = outs[:-1], outs[-1]
        n_pat = len(DILATIONS)
        q_d, k_d, v_d = outs[0:n_pat], outs[n_pat:2 * n_pat], outs[2 * n_pat:3 * n_pat]
        kt, vt, zs, u, gza, bz, uc = outs[3 * n_pat:]
        attn = _attention(list(zip(q_d, k_d, v_d)), bp, sp)
        aw8 = jnp.broadcast_to(a_conv_w[l][:, None, :], (CONV_WIDTH, SUBLANES, D_CONV))
        hp = _out_proj(hp, attn, zs, u, gza, bz, uc, aw8, *conv_w[1:], w_out_bf[l],
                       PROMPT_TM_OUT)
        pk.append(kt)
        pv.append(vt)
        pa.append(u[:, sp - (CONV_WIDTH - 1):])
        pc.append(uc[:, sp - (SCONV_WIDTH - 1):])

    return (hp, hs.reshape(bd, ts, D_MODEL),
            _from_channel_major(jnp.stack(pk)), _from_channel_major(jnp.stack(pv)),
            jnp.stack(pa), jnp.stack(pc),
            _from_channel_major(shifted[depth - 2]), _from_channel_major(shifted[depth - 1]),
            jnp.stack(sa), jnp.stack(sc))
```

```python
import functools

import numpy as np
import jax
import jax.numpy as jnp
from jax import lax
from jax.experimental import pallas as pl
from jax.experimental.pallas import tpu as pltpu

D_MODEL = 1024
N_HEADS = 8
HEAD_DIM = 64
D_ATTN = N_HEADS * HEAD_DIM
D_CONV = 256
CONV_WIDTH = 31
D_SCONV = 256
SCONV_WIDTH = 3
WINDOWS = (128, 512, 2048)
DILATIONS = (1, 4, 16)
WIN_KEYS = 128
assert all(w // d == WIN_KEYS for w, d in zip(WINDOWS, DILATIONS))
assert all(d & (d - 1) == 0 for d in DILATIONS)
assert len(DILATIONS) == 3 and DILATIONS[0] == 1 and DILATIONS[2] % DILATIONS[1] == 0
PAST_LEN = 16384
ROPE_THETA = 10000.0
EPS = 1e-6
NEG = -1e30
LOG2_E = 1.4426950408889634

C_Q, C_K, C_V, C_Z = 0, 512, 1024, 1536
C_AVAL, C_AGATE, C_ZA = 2048, 2304, 2560
C_CB, C_CC, C_CH, C_ZC = 2816, 3072, 3328, 3584
D_IN = 3840

LANES = 128
SUBLANES = 8
N_SLABS = D_ATTN // LANES
VMEM_LIMIT = 56 * 1024 * 1024

PROMPT_TM_IN = 512
PROMPT_TM_OUT = 512
KT_DILATIONS = tuple(d for d in DILATIONS if (PROMPT_TM_IN // d) % LANES == 0)

F32 = jnp.float32
BF16 = jnp.bfloat16


def _layer_spec(a, layer, **kwargs):
    return pl.BlockSpec((None,) + a.shape[1:], lambda *_: (layer,) + (0,) * (a.ndim - 1), **kwargs)


def _sigmoid(x):
    return 1.0 / (1.0 + jnp.exp(-x))


def _silu(x):
    return x * _sigmoid(x)


SHIFT_ROWS = 32


def _shift_cache_block(c_ref, nt_ref, n_ref, b):
    n_rows, l_cache = c_ref.shape
    bd = nt_ref.shape[1]
    mine = lax.broadcasted_iota(jnp.int32, (SHIFT_ROWS, bd), 1) == b
    last_lane = lax.broadcasted_iota(jnp.int32, (SHIFT_ROWS, LANES), 1) == LANES - 1
    for c in range(n_rows // SHIFT_ROWS):
        rows = slice(c * SHIFT_ROWS, (c + 1) * SHIFT_ROWS)
        new_col = jnp.sum(jnp.where(mine, nt_ref[rows, :], 0.0), axis=1, keepdims=True)
        rolled = pltpu.roll(c_ref[rows, :], l_cache - 1, 1)
        n_ref[rows, 0:l_cache - LANES] = rolled[:, 0:l_cache - LANES]
        n_ref[rows, l_cache - LANES:] = jnp.where(last_lane, new_col, rolled[:, l_cache - LANES:])


def _shift_specs(depth, bd, l_cache, step_of):
    cache = pl.BlockSpec((None, None, D_ATTN, l_cache),
                         lambda *g: (step_of(*g) // bd, step_of(*g) % bd, 0, 0))
    cols = pl.BlockSpec((None, D_ATTN, bd), lambda *g: (step_of(*g) // bd, 0, 0))
    return cache, cols


def _in_proj_body(x_ref, lng_ref, w_ref, cos_ref, sin_ref, qg_ref, kg_ref, seg_ref,
                  zs_ref, u_ref, gza_ref, bz_ref, uc_ref, emit_q, emit_k, emit_v, q_scale):
    x = x_ref[...].astype(F32)
    ms = jnp.mean(x * x, axis=-1, keepdims=True)
    xn = (x * lax.rsqrt(ms + EPS) * lng_ref[...]).astype(BF16)

    def proj(c0, c1):
        return jnp.dot(xn, w_ref[:, c0:c1], preferred_element_type=F32)

    cos = cos_ref[...]
    sin = sin_ref[...]
    tm = x.shape[0]
    first_half = (lax.broadcasted_iota(jnp.int32, (tm, LANES), 1) & (HEAD_DIM // 2)) == 0

    def norm_rope(p, g_ref, scale, emit):
        for s in range(N_SLABS):
            ps = p[:, s * LANES:(s + 1) * LANES]
            ss = jnp.dot((ps * ps).astype(BF16), seg_ref[...], preferred_element_type=F32)
            pn = ps * lax.rsqrt(ss + EPS) * g_ref[...]
            partner = jnp.where(first_half,
                                pltpu.roll(pn, LANES - HEAD_DIM // 2, 1),
                                pltpu.roll(pn, HEAD_DIM // 2, 1))
            emit(s, (pn * cos + partner * sin) * scale)

    norm_rope(proj(C_Q, C_K), qg_ref, q_scale, emit_q)
    norm_rope(proj(C_K, C_V), kg_ref, 1.0, emit_k)
    v = proj(C_V, C_Z)
    for s in range(N_SLABS):
        emit_v(s, v[:, s * LANES:(s + 1) * LANES])
    zs_ref[...] = _silu(proj(C_Z, C_AVAL)).astype(zs_ref.dtype)
    ag = proj(C_AVAL, C_ZA)
    u_ref[...] = ag[:, :D_CONV] * _sigmoid(ag[:, D_CONV:])
    zb = proj(C_ZA, C_CC)
    gza_ref[...] = _silu(zb[:, :D_CONV]).astype(gza_ref.dtype)
    zc = proj(C_ZC, D_IN)
    bz_ref[...] = (zb[:, D_CONV:] * _silu(zc)).astype(bz_ref.dtype)
    ch = proj(C_CC, C_ZC)
    uc_ref[...] = ch[:, :D_SCONV] * ch[:, D_SCONV:]


def _in_proj_prompt_kernel(x_ref, lng_ref, w_ref, cos_ref, sin_ref, qg_ref, kg_ref, seg_ref,
                           *rest, tm, shift):
    if shift:
        (c_ref, nt_ref, q1_ref, q4_ref, q16_ref, k1_ref, k4_ref, k16_ref, v1_ref, v4_ref, v16_ref,
         kt_ref, vt_ref, zs_ref, u_ref, gza_ref, bz_ref, uc_ref, n_ref,
         stage_ref, stage_next_ref) = rest
        grid_step = pl.program_id(0) * pl.num_programs(1) + pl.program_id(1)
        _shift_cache_block(c_ref, nt_ref, n_ref, grid_step % nt_ref.shape[1])
    else:
        (q1_ref, q4_ref, q16_ref, k1_ref, k4_ref, k16_ref, v1_ref, v4_ref, v16_ref,
         kt_ref, vt_ref, zs_ref, u_ref, gza_ref, bz_ref, uc_ref,
         stage_ref, stage_next_ref) = rest

    def emitter(which, outs, t_ref, transposed=()):
        def emit(s, val):
            slot = (which * N_SLABS + s) % 2
            levels = {1: stage_ref.at[slot]}
            levels[1][0] = val
            val_t = val.T if t_ref is not None else None
            for li, (d, out) in enumerate(zip(DILATIONS, outs)):
                prev = DILATIONS[li - 1] if li else 1
                keep = li + 1 < len(DILATIONS)
                for r in range(d):
                    if d == 1:
                        part = val
                    else:
                        part = levels[prev][r % prev, pl.ds(r // prev, tm // d, stride=d // prev), :]
                        if keep:
                            stage_next_ref[slot, r] = part
                    if d in transposed:
                        part = val_t if d == 1 else part.T
                    out[s, r] = part.astype(BF16)
                if d != 1 and keep:
                    levels[d] = stage_next_ref.at[slot]
            if t_ref is not None:
                t_ref[s * LANES:(s + 1) * LANES, :] = val_t
        return emit

    _in_proj_body(x_ref, lng_ref, w_ref, cos_ref, sin_ref, qg_ref, kg_ref, seg_ref,
                  zs_ref, u_ref, gza_ref, bz_ref, uc_ref,
                  emitter(0, (q1_ref, q4_ref, q16_ref), None),
                  emitter(1, (k1_ref, k4_ref, k16_ref), kt_ref, transposed=KT_DILATIONS),
                  emitter(2, (v1_ref, v4_ref, v16_ref), vt_ref),
                  q_scale=HEAD_DIM ** -0.5 * LOG2_E)


def _in_proj_prompt(x, lng, w_bf, cos_t, sin_t, qg, kg, seg, layer, tm, l_tail, shift=None):
    b, s, _ = x.shape
    grid = (b, s // tm)
    n_skip = (s - l_tail) // tm
    tile = lambda c: pl.BlockSpec((None, tm, c), lambda bi, i: (bi, i, 0))
    full = lambda a: pl.BlockSpec(a.shape, lambda bi, i: (0,) * a.ndim)
    pos = pl.BlockSpec((tm, LANES), lambda bi, i: (i, 0))
    def deint(d, transposed):
        blk = (LANES, tm // d) if transposed else (tm // d, LANES)
        idx = (lambda bi, i: (bi, 0, 0, 0, i)) if transposed else (lambda bi, i: (bi, 0, 0, i, 0))
        dims = (LANES, s // d) if transposed else (s // d, LANES)
        return (pl.BlockSpec((None, N_SLABS, d) + blk, idx),
                jax.ShapeDtypeStruct((b, N_SLABS, d) + dims, BF16))

    qkv = [deint(d, which == 1 and d in KT_DILATIONS) for which in range(3) for d in DILATIONS]
    tail = pl.BlockSpec((None, D_ATTN, tm), lambda bi, i: (bi, 0, jnp.maximum(i - n_skip, 0)))
    out_shapes = tuple(shape for _, shape in qkv) + (
        jax.ShapeDtypeStruct((b, D_ATTN, l_tail), F32),
        jax.ShapeDtypeStruct((b, D_ATTN, l_tail), F32),
        jax.ShapeDtypeStruct((b, s, D_ATTN), BF16),
        jax.ShapeDtypeStruct((b, s, D_CONV), F32),
        jax.ShapeDtypeStruct((b, s, D_CONV), BF16),
        jax.ShapeDtypeStruct((b, s, D_SCONV), BF16),
        jax.ShapeDtypeStruct((b, s, D_SCONV), F32),
    )
    out_specs = tuple(spec for spec, _ in qkv) + (
        tail, tail, tile(D_ATTN), tile(D_CONV), tile(D_CONV), tile(D_SCONV), tile(D_SCONV))
    w_spec = _layer_spec(w_bf, layer, pipeline_mode=pl.Buffered(1))
    in_specs = [tile(D_MODEL), _layer_spec(lng, layer), w_spec, pos, pos,
                _layer_spec(qg, layer), _layer_spec(kg, layer), full(seg)]
    args = [x, lng, w_bf, cos_t, sin_t, qg, kg, seg]
    if shift is not None:
        cache, cols = shift
        depth, bd, _, l_cache = cache.shape
        assert depth * bd == grid[0] * grid[1]
        cache_spec, cols_spec = _shift_specs(depth, bd, l_cache, lambda bi, i: bi * grid[1] + i)
        in_specs += [cache_spec, cols_spec]
        args += [cache, cols]
        out_specs += (cache_spec,)
        out_shapes += (jax.ShapeDtypeStruct(cache.shape, F32),)
    return pl.pallas_call(
        functools.partial(_in_proj_prompt_kernel, tm=tm, shift=shift is not None),
        grid=grid,
        in_specs=in_specs,
        out_specs=out_specs,
        out_shape=out_shapes,
        scratch_shapes=[pltpu.VMEM((2, 1, tm, LANES), F32),
                        pltpu.VMEM((2, DILATIONS[1], tm // DILATIONS[1], LANES), F32)],
        compiler_params=pltpu.CompilerParams(dimension_semantics=("arbitrary", "arbitrary"),
                                             vmem_limit_bytes=VMEM_LIMIT),
        name="in_proj",
    )(*args)


def _in_proj_sample_kernel(x_ref, lng_ref, w_ref, cos_ref, sin_ref, qg_ref, kg_ref, seg_ref,
                           q_ref, k_ref, v_ref, zs_ref, u_ref, gza_ref, bz_ref, uc_ref):
    def emitter(out):
        def emit(s, val):
            out[:, s * LANES:(s + 1) * LANES] = val
        return emit

    _in_proj_body(x_ref, lng_ref, w_ref, cos_ref, sin_ref, qg_ref, kg_ref, seg_ref,
                  zs_ref, u_ref, gza_ref, bz_ref, uc_ref,
                  emitter(q_ref), emitter(k_ref), emitter(v_ref), q_scale=HEAD_DIM ** -0.5)


def _in_proj_sample(x2d, lng, w_bf, cos_t, sin_t, qg, kg, seg, layer):
    n = x2d.shape[0]
    args = (x2d, lng, w_bf, cos_t, sin_t, qg, kg, seg)
    full = lambda a: pl.BlockSpec(a.shape, lambda i: (0,) * a.ndim)
    per_layer = lambda a: _layer_spec(a, layer)
    widths = (D_ATTN, D_ATTN, D_ATTN, D_ATTN, D_CONV, D_CONV, D_SCONV, D_SCONV)
    out_shapes = tuple(jax.ShapeDtypeStruct((n, c), F32) for c in widths)
    return pl.pallas_call(
        _in_proj_sample_kernel,
        grid=(1,),
        in_specs=[full(x2d), per_layer(lng), per_layer(w_bf), full(cos_t), full(sin_t),
                  per_layer(qg), per_layer(kg), full(seg)],
        out_specs=tuple(full(s) for s in out_shapes),
        out_shape=out_shapes,
        compiler_params=pltpu.CompilerParams(dimension_semantics=("arbitrary",),
                                             vmem_limit_bytes=VMEM_LIMIT),
        name="in_proj_sample",
    )(*args)


ATTN_TILE = 2048
assert all(ATTN_TILE % (WIN_KEYS * d) == 0 for d in DILATIONS)
ATTN_UNROLL = 8
assert (ATTN_TILE // WIN_KEYS) % ATTN_UNROLL == 0


def _attn_kernel(*refs, tile):
    n_pat = len(DILATIONS)
    qkv_refs = [refs[3 * p:3 * p + 3] for p in range(n_pat)]
    bias_ref, o_ref, acc_ref, m_ref, l_ref = refs[3 * n_pat:]
    t = pl.program_id(2)
    blk = WIN_KEYS
    n_blocks = tile // blk
    lane = lax.broadcasted_iota(jnp.int32, (1, LANES), 1)
    head_a = lane < HEAD_DIM

    def block_softmax(qb, k_ref, v_ref, r, q0, k_channel_major):
        kk = pl.multiple_of(jnp.maximum(q0 - blk, 0), blk)
        bias = bias_ref[jnp.minimum(q0, 1)]
        if k_channel_major:
            kb = k_ref[r, :, pl.ds(kk, 2 * blk)]
            contract = (((1,), (0,)), ((), ()))
        else:
            kb = k_ref[r, pl.ds(kk, 2 * blk), :]
            contract = (((1,), (1,)), ((), ()))
        vb = v_ref[r, pl.ds(kk, 2 * blk), :]
        res, mx = [], []
        for sel in (head_a, jnp.logical_not(head_a)):
            qh = jnp.where(sel, qb, jnp.zeros_like(qb))
            s = lax.dot_general(qh, kb, contract, preferred_element_type=F32)
            s = s + bias
            m = jnp.max(s, axis=1, keepdims=True)
            p = jnp.exp2(s - m).astype(BF16)
            vh = jnp.where(sel, vb, jnp.ones_like(vb))
            res.append(jnp.dot(p, vh, preferred_element_type=F32))
            mx.append(m)
        num = jnp.where(head_a, res[0], res[1])
        den = pltpu.roll(jnp.where(head_a, res[1], res[0]), HEAD_DIM, 1)
        return num, den, jnp.where(head_a, mx[0], mx[1])

    order = sorted(range(n_pat), key=lambda p: -DILATIONS[p])
    for step, p in enumerate(order):
        d = DILATIONS[p]
        q_ref, k_ref, v_ref = qkv_refs[p]
        per_res = n_blocks // d
        assert per_res & (per_res - 1) == 0
        first, last = step == 0, step == n_pat - 1

        def body(it, carry, d=d, q_ref=q_ref, k_ref=k_ref, v_ref=v_ref, per_res=per_res,
                 first=first, last=last):
            for u in range(ATTN_UNROLL):
                i = it * ATTN_UNROLL + u
                r = lax.shift_right_logical(i, per_res.bit_length() - 1)
                jb = i & (per_res - 1)
                qb = q_ref[r, pl.ds(pl.multiple_of(jb * blk, blk), blk), :]
                num, den, mb = block_softmax(qb, k_ref, v_ref, r, t * (tile // d) + jb * blk,
                                             d in KT_DILATIONS)
                start = jb * blk * d + r
                rows = pl.ds(start, blk) if d == 1 else pl.ds(start, blk, stride=d)
                if not first:
                    m_old = m_ref[rows, :]
                    m_new = jnp.maximum(m_old, mb)
                    w_old = jnp.exp2(m_old - m_new)
                    w_blk = jnp.exp2(mb - m_new)
                    num = acc_ref[rows, :] * w_old + num * w_blk
                    den = l_ref[rows, :] * w_old + den * w_blk
                    mb = m_new
                if last:
                    o_ref[rows, :] = num / den
                else:
                    acc_ref[rows, :] = num
                    l_ref[rows, :] = den
                    m_ref[rows, :] = mb
            return carry

        lax.fori_loop(0, n_blocks // ATTN_UNROLL, body, 0)


def _band_bias():
    rel = np.arange(2 * WIN_KEYS)[None, :] - np.arange(WIN_KEYS)[:, None]
    variants = [rel - off for off in (0, WIN_KEYS)]
    return jnp.asarray(np.stack([np.where((v <= 0) & (v >= -WIN_KEYS), 0.0, NEG)
                                 for v in variants]), dtype=F32)


def _attention(qkv, b, s):
    tile = ATTN_TILE
    in_specs, args = [], []
    for d, (q, k, v) in zip(DILATIONS, qkv):
        qspec = pl.BlockSpec((None, None, d, tile // d, LANES), lambda bi, c, i: (bi, c, 0, i, 0))
        kvspec = pl.BlockSpec((None, None, d, s // d, LANES), lambda bi, c, i: (bi, c, 0, 0, 0))
        ktspec = pl.BlockSpec((None, None, d, LANES, s // d), lambda bi, c, i: (bi, c, 0, 0, 0))
        in_specs += [qspec, ktspec if d in KT_DILATIONS else kvspec, kvspec]
        args += [q, k, v]
    bias = _band_bias()
    in_specs.append(pl.BlockSpec(bias.shape, lambda bi, c, i: (0, 0, 0)))
    args.append(bias)
    return pl.pallas_call(
        functools.partial(_attn_kernel, tile=tile),
        grid=(b, N_SLABS, s // tile),
        in_specs=in_specs,
        out_specs=pl.BlockSpec((None, tile, LANES), lambda bi, c, i: (bi, i, c)),
        out_shape=jax.ShapeDtypeStruct((b, s, D_ATTN), F32),
        scratch_shapes=[pltpu.VMEM((tile, LANES), F32)] * 3,
        compiler_params=pltpu.CompilerParams(
            dimension_semantics=("arbitrary", "arbitrary", "arbitrary"),
            vmem_limit_bytes=VMEM_LIMIT),
        name="dilated_attn",
    )(*args)


A_HALO = 32
C_HALO = 8
OUT_CHUNK = 128


def _layernorm_rows(x, g, b):
    mu = jnp.mean(x, axis=-1, keepdims=True)
    xc = x - mu
    var = jnp.mean(xc * xc, axis=-1, keepdims=True)
    return xc * lax.rsqrt(var + EPS) * g + b


def _out_proj_kernel(x_ref, attn_ref, zs_ref, u_ref, uh_ref, gza_ref, bz_ref, uc_ref, uch_ref,
                     aw_ref, ab_ref, ag_ref, alb_ref, cw_ref, w_ref,
                     y_ref, *rest, tm):
    ubuf, uphase, ucbuf = rest[-3:]
    first_tile = pl.program_id(1) == 0
    y_ref[...] = x_ref[...] + jnp.dot((attn_ref[...] * zs_ref[...]).astype(BF16),
                                      w_ref[0:D_ATTN, :], preferred_element_type=F32)

    ubuf[0:A_HALO, :] = jnp.where(first_tile, 0.0, uh_ref[...])
    ubuf[A_HALO:A_HALO + tm, :] = u_ref[...]
    n_rows = tm + A_HALO - SUBLANES
    for b in range(1, SUBLANES):
        uphase[b - 1] = ubuf[b:b + n_rows, :]
    ucbuf[0:C_HALO, :] = jnp.where(first_tile, 0.0, uch_ref[...])
    ucbuf[C_HALO:C_HALO + tm, :] = uc_ref[...]

    base = A_HALO - (CONV_WIDTH - 1)
    cbase = C_HALO - (SCONV_WIDTH - 1)
    groups = (OUT_CHUNK // SUBLANES, SUBLANES, D_CONV)
    for c in range(tm // OUT_CHUNK):
        r0 = c * OUT_CHUNK
        rows = slice(r0, r0 + OUT_CHUNK)
        ca = jnp.zeros(groups, F32)
        for j in range(CONV_WIDTH):
            b = (base + j) % SUBLANES
            a8 = base + j - b + r0
            src = (ubuf[a8:a8 + OUT_CHUNK, :] if b == 0
                   else uphase[b - 1, a8:a8 + OUT_CHUNK, :])
            ca = ca + aw_ref[j][None] * src.reshape(groups)
        ca = ca.reshape(OUT_CHUNK, D_CONV) + ab_ref[...]
        ya = _silu(_layernorm_rows(ca, ag_ref[...], alb_ref[...])) * gza_ref[rows, :]
        cc = jnp.zeros((OUT_CHUNK, D_SCONV), F32)
        for j in range(SCONV_WIDTH):
            cc = cc + cw_ref[j:j + 1, :] * ucbuf[cbase + j + r0:cbase + j + r0 + OUT_CHUNK, :]
        yc = bz_ref[rows, :] * cc
        y = y_ref[rows, :] + (
            jnp.dot(ya.astype(BF16), w_ref[D_ATTN:D_ATTN + D_CONV, :], preferred_element_type=F32)
            + jnp.dot(yc.astype(BF16), w_ref[D_ATTN + D_CONV:, :], preferred_element_type=F32))
        y_ref[rows, :] = y
        if len(rest) > 3:
            rest[0][rows, :] = y.astype(BF16)


def _out_proj(x, attn, zs, u, gza, bz, uc, aw, ab, ag, alb, cw, w_bf, layer, tm, bf16_copy):
    b, s, _ = x.shape
    tile = lambda c: pl.BlockSpec((None, tm, c), lambda bi, i: (bi, i, 0))
    halo = lambda h, c: pl.BlockSpec(
        (None, h, c), lambda bi, i: (bi, jnp.maximum(i * (tm // h) - 1, 0), 0))
    out_specs = (tile(D_MODEL),) + ((tile(D_MODEL),) if bf16_copy else ())
    out_shape = (jax.ShapeDtypeStruct((b, s, D_MODEL), F32),) + (
        (jax.ShapeDtypeStruct((b, s, D_MODEL), BF16),) if bf16_copy else ())
    return pl.pallas_call(
        functools.partial(_out_proj_kernel, tm=tm),
        grid=(b, s // tm),
        in_specs=[tile(D_MODEL), tile(D_ATTN), tile(D_ATTN),
                  tile(D_CONV), halo(A_HALO, D_CONV), tile(D_CONV), tile(D_SCONV),
                  tile(D_SCONV), halo(C_HALO, D_SCONV)]
                 + [_layer_spec(a, layer) for a in (aw, ab, ag, alb, cw, w_bf)],
        out_specs=out_specs,
        out_shape=out_shape,
        scratch_shapes=[pltpu.VMEM((A_HALO + tm, D_CONV), F32),
                        pltpu.VMEM((SUBLANES - 1, A_HALO + tm - SUBLANES, D_CONV), F32),
                        pltpu.VMEM((C_HALO + tm, D_SCONV), F32)],
        compiler_params=pltpu.CompilerParams(dimension_semantics=("arbitrary", "arbitrary"),
                                             vmem_limit_bytes=VMEM_LIMIT),
        name="out_proj",
    )(x, attn, zs, u, u, gza, bz, uc, uc, aw, ab, ag, alb, cw, w_bf)


def _sample_attn_kernel(q_ref, kn_ref, vn_ref, kt_ref, vt_ref, attn_ref):
    l_cache = kt_ref.shape[1]
    kn = kn_ref[...]
    vn = vn_ref[...]
    head_of_lane = lax.broadcasted_iota(jnp.int32, (N_HEADS, D_ATTN), 1) // HEAD_DIM
    own = head_of_lane == lax.broadcasted_iota(jnp.int32, (N_HEADS, D_ATTN), 0)
    qblk = jnp.where(own, q_ref[...], 0.0)
    s = jnp.dot(qblk.astype(BF16), kt_ref[...].astype(BF16), preferred_element_type=F32)
    s_self = jnp.sum(qblk * kn, axis=1, keepdims=True)

    dist = l_cache - lax.broadcasted_iota(jnp.int32, (N_HEADS, l_cache), 1)
    cnt = jnp.zeros((N_HEADS, l_cache), F32)
    for w, d in zip(WINDOWS, DILATIONS):
        cnt = cnt + jnp.where((dist <= w) & ((dist & (d - 1)) == 0), 1.0, 0.0)
    s = jnp.where(cnt > 0, s, NEG)
    m = jnp.maximum(jnp.max(s, axis=1, keepdims=True), s_self)
    e = (cnt * jnp.exp(s - m)).astype(BF16)
    e_self = len(WINDOWS) * jnp.exp(s_self - m)
    den = jnp.sum(e.astype(F32), axis=1, keepdims=True) + e_self
    pv = lax.dot_general(e, vt_ref[...].astype(BF16), (((1,), (1,)), ((), ())),
                         preferred_element_type=F32)
    per_lane = lambda a: jnp.sum(jnp.where(own, a, 0.0), axis=0, keepdims=True)
    num = per_lane(pv) + per_lane(jnp.broadcast_to(e_self, own.shape)) * vn
    attn_ref[...] = num / per_lane(jnp.broadcast_to(den, own.shape))


def _sample_attn(q, kf, vf, cache_kt, cache_vt, layer):
    _, bd, _, l_cache = cache_kt.shape
    assert l_cache >= WINDOWS[-1]
    row = pl.BlockSpec((None, 1, D_ATTN), lambda b: (b, 0, 0))
    cache = pl.BlockSpec((None, None, D_ATTN, l_cache), lambda b: (layer, b, 0, 0))
    r3 = lambda a: a.reshape(bd, 1, D_ATTN)
    attn = pl.pallas_call(
        _sample_attn_kernel,
        grid=(bd,),
        in_specs=[row, row, row, cache, cache],
        out_specs=row,
        out_shape=jax.ShapeDtypeStruct((bd, 1, D_ATTN), F32),
        compiler_params=pltpu.CompilerParams(dimension_semantics=("arbitrary",),
                                             vmem_limit_bytes=VMEM_LIMIT),
        name="sample_attn",
    )(r3(q), r3(kf), r3(vf), cache_kt, cache_vt)
    return attn.reshape(bd, D_ATTN)


def _sample_out_kernel(x_ref, attn_ref, zs_ref, u_ref, gza_ref, bz_ref, uc_ref, sa_ref, sc_ref,
                       aw_ref, ab_ref, ag_ref, alb_ref, cw_ref, w_ref, y_ref, na_ref, nc_ref):
    u = u_ref[...]
    uc = uc_ref[...]
    na = CONV_WIDTH - 1
    nc = SCONV_WIDTH - 1
    ca = ab_ref[...] + aw_ref[na:na + 1, :] * u
    for j in range(na):
        ca = ca + aw_ref[j:j + 1, :] * sa_ref[:, j * D_CONV:(j + 1) * D_CONV]
    ya = _silu(_layernorm_rows(ca, ag_ref[...], alb_ref[...])) * gza_ref[...]
    cc = cw_ref[nc:nc + 1, :] * uc
    for j in range(nc):
        cc = cc + cw_ref[j:j + 1, :] * sc_ref[:, j * D_SCONV:(j + 1) * D_SCONV]
    yc = bz_ref[...] * cc
    acc = jnp.dot((attn_ref[...] * zs_ref[...]).astype(BF16), w_ref[0:D_ATTN, :],
                  preferred_element_type=F32)
    acc = acc + jnp.dot(ya.astype(BF16), w_ref[D_ATTN:D_ATTN + D_CONV, :],
                        preferred_element_type=F32)
    acc = acc + jnp.dot(yc.astype(BF16), w_ref[D_ATTN + D_CONV:, :],
                        preferred_element_type=F32)
    y_ref[...] = x_ref[...] + acc
    na_ref[:, 0:(na - 1) * D_CONV] = sa_ref[:, D_CONV:na * D_CONV]
    na_ref[:, (na - 1) * D_CONV:] = u
    nc_ref[:, 0:(nc - 1) * D_SCONV] = sc_ref[:, D_SCONV:nc * D_SCONV]
    nc_ref[:, (nc - 1) * D_SCONV:] = uc


def _sample_out(x, attn, zs, u, gza, bz, uc, sa, sc, aw, ab, ag, alb, cw, w_bf, layer):
    depth, bd = sa.shape[:2]
    sa2 = sa.reshape(depth, bd, -1)
    sc2 = sc.reshape(depth, bd, -1)
    per_token = (x, attn, zs, u, gza, bz, uc)
    per_layer = (sa2, sc2, aw, ab, ag, alb, cw, w_bf)
    full = lambda a: pl.BlockSpec(a.shape, lambda i: (0,) * a.ndim)
    y, na, nc = pl.pallas_call(
        _sample_out_kernel,
        grid=(1,),
        in_specs=[full(a) for a in per_token] + [_layer_spec(a, layer) for a in per_layer],
        out_specs=(full(x), pl.BlockSpec(sa2.shape[1:], lambda i: (0, 0)),
                   pl.BlockSpec(sc2.shape[1:], lambda i: (0, 0))),
        out_shape=(jax.ShapeDtypeStruct(x.shape, F32),
                   jax.ShapeDtypeStruct(sa2.shape[1:], F32),
                   jax.ShapeDtypeStruct(sc2.shape[1:], F32)),
        compiler_params=pltpu.CompilerParams(dimension_semantics=("arbitrary",),
                                             vmem_limit_bytes=VMEM_LIMIT),
        name="sample_out",
    )(*per_token, *per_layer)
    return y, na.reshape(sa.shape[1:]), nc.reshape(sc.shape[1:])


def _rope_tables(pos):
    half = HEAD_DIM // 2
    inv = ROPE_THETA ** (-jnp.arange(half, dtype=F32) / half)
    ang = pos.astype(F32)[:, None] * inv[None, :]
    cos, sin = jnp.cos(ang), jnp.sin(ang)
    reps = LANES // HEAD_DIM
    cos_t = jnp.tile(jnp.concatenate([cos, cos], axis=-1), (1, reps))
    sin_t = jnp.tile(jnp.concatenate([-sin, sin], axis=-1), (1, reps))
    return cos_t, sin_t


def _segment_mean_matrix():
    seg = np.kron(np.eye(LANES // HEAD_DIM), np.ones((HEAD_DIM, HEAD_DIM))) / HEAD_DIM
    return jnp.asarray(seg, dtype=BF16)


def _to_channel_major(cache):
    lead = cache.shape[:-3]
    n = len(lead)
    perm = tuple(range(n)) + (n + 1, n + 2, n)
    return jnp.transpose(cache, perm).reshape(*lead, D_ATTN, cache.shape[-3])


def _from_channel_major(cache_t):
    lead = cache_t.shape[:-2]
    n = len(lead)
    perm = tuple(range(n)) + (n + 2, n, n + 1)
    return jnp.transpose(cache_t.reshape(*lead, N_HEADS, HEAD_DIM, cache_t.shape[-1]), perm)


def kernel(x_prompt, x_sample, cache_k, cache_v, state_conv_a, state_conv_c, ln_g, w_in,
           q_norm_g, k_norm_g, a_conv_w, a_conv_b, a_ln_g, a_ln_b, c_conv_w, w_out):
    bp, sp, _ = x_prompt.shape
    bd, ts, _ = x_sample.shape
    assert ts == 1
    depth = w_in.shape[0]
    l_prompt = min(WINDOWS[-1], sp)

    cos_p, sin_p = _rope_tables(jnp.arange(sp, dtype=jnp.int32))
    cos_s, sin_s = _rope_tables(jnp.full((bd,), PAST_LEN, dtype=jnp.int32))
    seg = _segment_mean_matrix()
    cache_kt = _to_channel_major(cache_k)
    cache_vt = _to_channel_major(cache_v)

    rows = lambda a: a.reshape(depth, 1, -1)
    w_in_bf = w_in.astype(BF16)
    w_out_bf = w_out.astype(BF16)
    lng = rows(ln_g)
    qg = rows(jnp.tile(q_norm_g, (1, LANES // HEAD_DIM)))
    kg = rows(jnp.tile(k_norm_g, (1, LANES // HEAD_DIM)))
    conv_w = (a_conv_w, rows(a_conv_b), rows(a_ln_g), rows(a_ln_b), c_conv_w)
    aw8 = jnp.broadcast_to(a_conv_w[:, :, None, :], (depth, CONV_WIDTH, SUBLANES, D_CONV))

    hs = x_sample.reshape(bd, D_MODEL)
    sk, sv, sa, sc = [], [], [], []
    for l in range(depth):
        q, kf, vf, zs, u, gza, bz, uc = _in_proj_sample(hs, lng, w_in_bf, cos_s, sin_s, qg, kg,
                                                        seg, l)
        attn = _sample_attn(q, kf, vf, cache_kt, cache_vt, l)
        hs, na, nc = _sample_out(hs, attn, zs, u, gza, bz, uc, state_conv_a, state_conv_c,
                                 *conv_w, w_out_bf, l)
        sk.append(kf)
        sv.append(vf)
        sa.append(na)
        sc.append(nc)
    new_cols = lambda new_rows: jnp.transpose(jnp.stack(new_rows), (0, 2, 1))

    assert depth >= 2
    shift_jobs = {depth - 2: (cache_kt, new_cols(sk)), depth - 1: (cache_vt, new_cols(sv))}
    shifted = {}

    hp = hp_in = x_prompt
    pk, pv, pa, pc = [], [], [], []
    for l in range(depth):
        outs = _in_proj_prompt(hp_in, lng, w_in_bf, cos_p, sin_p, qg, kg, seg, l, PROMPT_TM_IN,
                               l_prompt, shift=shift_jobs.get(l))
        if l in shift_jobs:
            outs, shifted[l] = outs[:-1], outs[-1]
        n_pat = len(DILATIONS)
        q_d, k_d, v_d = outs[0:n_pat], outs[n_pat:2 * n_pat], outs[2 * n_pat:3 * n_pat]
        kt, vt, zs, u, gza, bz, uc = outs[3 * n_pat:]
        attn = _attention(list(zip(q_d, k_d, v_d)), bp, sp)
        hp, *copy = _out_proj(hp, attn, zs, u, gza, bz, uc, aw8, *conv_w[1:], w_out_bf, l,
                              PROMPT_TM_OUT, bf16_copy=l + 1 < depth)
        hp_in = copy[0] if copy else hp
        pk.append(kt)
        pv.append(vt)
        pa.append(u[:, sp - (CONV_WIDTH - 1):])
        pc.append(uc[:, sp - (SCONV_WIDTH - 1):])

    return (hp, hs.reshape(bd, ts, D_MODEL),
            _from_channel_major(jnp.stack(pk)), _from_channel_major(jnp.stack(pv)),
            jnp.stack(pa), jnp.stack(pc),
            _from_channel_major(shifted[depth - 2]), _from_channel_major(shifted[depth - 1]),
            jnp.stack(sa), jnp.stack(sc))
```

```python
import functools

import numpy as np
import jax
import jax.numpy as jnp
from jax import lax
from jax.experimental import pallas as pl
from jax.experimental.pallas import tpu as pltpu

D_MODEL = 1024
N_HEADS = 8
HEAD_DIM = 64
D_ATTN = N_HEADS * HEAD_DIM
D_CONV = 256
CONV_WIDTH = 31
D_SCONV = 256
SCONV_WIDTH = 3
WINDOWS = (128, 512, 2048)
DILATIONS = (1, 4, 16)
WIN_KEYS = 128
assert all(w // d == WIN_KEYS for w, d in zip(WINDOWS, DILATIONS))
assert all(d & (d - 1) == 0 for d in DILATIONS)
assert len(DILATIONS) == 3 and DILATIONS[0] == 1 and DILATIONS[2] % DILATIONS[1] == 0
PAST_LEN = 16384
ROPE_THETA = 10000.0
EPS = 1e-6
NEG = -1e30
LOG2_E = 1.4426950408889634

C_Q, C_K, C_V, C_Z = 0, 512, 1024, 1536
C_AVAL, C_AGATE, C_ZA = 2048, 2304, 2560
C_CB, C_CC, C_CH, C_ZC = 2816, 3072, 3328, 3584
D_IN = 3840

LANES = 128
SUBLANES = 8
N_SLABS = D_ATTN // LANES
VMEM_LIMIT = 56 * 1024 * 1024

PROMPT_TM_IN = 512
PROMPT_TM_OUT = 512
KT_DILATIONS = tuple(d for d in DILATIONS if (PROMPT_TM_IN // d) % LANES == 0)

F32 = jnp.float32
BF16 = jnp.bfloat16


def _layer_spec(a, layer, **kwargs):
    return pl.BlockSpec((None,) + a.shape[1:], lambda *_: (layer,) + (0,) * (a.ndim - 1), **kwargs)


def _sigmoid(x):
    return 1.0 / (1.0 + jnp.exp(-x))


def _silu(x):
    return x * _sigmoid(x)


SHIFT_ROWS = 32


def _shift_cache_block(c_ref, nt_ref, n_ref, b):
    n_rows, l_cache = c_ref.shape
    bd = nt_ref.shape[1]
    mine = lax.broadcasted_iota(jnp.int32, (SHIFT_ROWS, bd), 1) == b
    last_lane = lax.broadcasted_iota(jnp.int32, (SHIFT_ROWS, LANES), 1) == LANES - 1
    for c in range(n_rows // SHIFT_ROWS):
        rows = slice(c * SHIFT_ROWS, (c + 1) * SHIFT_ROWS)
        new_col = jnp.sum(jnp.where(mine, nt_ref[rows, :], 0.0), axis=1, keepdims=True)
        rolled = pltpu.roll(c_ref[rows, :], l_cache - 1, 1)
        n_ref[rows, 0:l_cache - LANES] = rolled[:, 0:l_cache - LANES]
        n_ref[rows, l_cache - LANES:] = jnp.where(last_lane, new_col, rolled[:, l_cache - LANES:])


def _shift_specs(depth, bd, l_cache, step_of):
    cache = pl.BlockSpec((None, None, D_ATTN, l_cache),
                         lambda *g: (step_of(*g) // bd, step_of(*g) % bd, 0, 0))
    cols = pl.BlockSpec((None, D_ATTN, bd), lambda *g: (step_of(*g) // bd, 0, 0))
    return cache, cols


def _in_proj_body(x_ref, lng_ref, w_ref, cos_ref, sin_ref, qg_ref, kg_ref, seg_ref,
                  zs_ref, u_ref, gza_ref, bz_ref, uc_ref, emit_q, emit_k, emit_v, q_scale):
    x = x_ref[...]
    ms = jnp.mean(x * x, axis=-1, keepdims=True)
    xn = (x * lax.rsqrt(ms + EPS) * lng_ref[...]).astype(BF16)

    def proj(c0, c1):
        return jnp.dot(xn, w_ref[:, c0:c1], preferred_element_type=F32)

    cos = cos_ref[...]
    sin = sin_ref[...]
    tm = x.shape[0]
    first_half = (lax.broadcasted_iota(jnp.int32, (tm, LANES), 1) & (HEAD_DIM // 2)) == 0

    def norm_rope(p, g_ref, scale, emit):
        for s in range(N_SLABS):
            ps = p[:, s * LANES:(s + 1) * LANES]
            ss = jnp.dot((ps * ps).astype(BF16), seg_ref[...], preferred_element_type=F32)
            pn = ps * lax.rsqrt(ss + EPS) * g_ref[...]
            partner = jnp.where(first_half,
                                pltpu.roll(pn, LANES - HEAD_DIM // 2, 1),
                                pltpu.roll(pn, HEAD_DIM // 2, 1))
            emit(s, (pn * cos + partner * sin) * scale)

    norm_rope(proj(C_Q, C_K), qg_ref, q_scale, emit_q)
    norm_rope(proj(C_K, C_V), kg_ref, 1.0, emit_k)
    v = proj(C_V, C_Z)
    for s in range(N_SLABS):
        emit_v(s, v[:, s * LANES:(s + 1) * LANES])
    zs_ref[...] = _silu(proj(C_Z, C_AVAL)).astype(zs_ref.dtype)
    ag = proj(C_AVAL, C_ZA)
    u_ref[...] = ag[:, :D_CONV] * _sigmoid(ag[:, D_CONV:])
    zb = proj(C_ZA, C_CC)
    gza_ref[...] = _silu(zb[:, :D_CONV]).astype(gza_ref.dtype)
    zc = proj(C_ZC, D_IN)
    bz_ref[...] = (zb[:, D_CONV:] * _silu(zc)).astype(bz_ref.dtype)
    ch = proj(C_CC, C_ZC)
    uc_ref[...] = ch[:, :D_SCONV] * ch[:, D_SCONV:]


def _in_proj_prompt_kernel(x_ref, lng_ref, w_ref, cos_ref, sin_ref, qg_ref, kg_ref, seg_ref,
                           *rest, tm, shift):
    if shift:
        (c_ref, nt_ref, q1_ref, q4_ref, q16_ref, k1_ref, k4_ref, k16_ref, v1_ref, v4_ref, v16_ref,
         kt_ref, vt_ref, zs_ref, u_ref, gza_ref, bz_ref, uc_ref, n_ref,
         stage_ref, stage_next_ref) = rest
        grid_step = pl.program_id(0) * pl.num_programs(1) + pl.program_id(1)
        _shift_cache_block(c_ref, nt_ref, n_ref, grid_step % nt_ref.shape[1])
    else:
        (q1_ref, q4_ref, q16_ref, k1_ref, k4_ref, k16_ref, v1_ref, v4_ref, v16_ref,
         kt_ref, vt_ref, zs_ref, u_ref, gza_ref, bz_ref, uc_ref,
         stage_ref, stage_next_ref) = rest

    def emitter(which, outs, t_ref, transposed=()):
        def emit(s, val):
            slot = (which * N_SLABS + s) % 2
            levels = {1: stage_ref.at[slot]}
            levels[1][0] = val
            val_t = val.T if t_ref is not None else None
            for li, (d, out) in enumerate(zip(DILATIONS, outs)):
                prev = DILATIONS[li - 1] if li else 1
                keep = li + 1 < len(DILATIONS)
                for r in range(d):
                    if d == 1:
                        part = val
                    else:
                        part = levels[prev][r % prev, pl.ds(r // prev, tm // d, stride=d // prev), :]
                        if keep:
                            stage_next_ref[slot, r] = part
                    if d in transposed:
                        part = val_t if d == 1 else part.T
                    out[s, r] = part.astype(BF16)
                if d != 1 and keep:
                    levels[d] = stage_next_ref.at[slot]
            if t_ref is not None:
                t_ref[s * LANES:(s + 1) * LANES, :] = val_t
        return emit

    _in_proj_body(x_ref, lng_ref, w_ref, cos_ref, sin_ref, qg_ref, kg_ref, seg_ref,
                  zs_ref, u_ref, gza_ref, bz_ref, uc_ref,
                  emitter(0, (q1_ref, q4_ref, q16_ref), None),
                  emitter(1, (k1_ref, k4_ref, k16_ref), kt_ref, transposed=KT_DILATIONS),
                  emitter(2, (v1_ref, v4_ref, v16_ref), vt_ref),
                  q_scale=HEAD_DIM ** -0.5 * LOG2_E)


def _in_proj_prompt(x, lng, w_bf, cos_t, sin_t, qg, kg, seg, layer, tm, l_tail, shift=None):
    b, s, _ = x.shape
    grid = (b, s // tm)
    n_skip = (s - l_tail) // tm
    tile = lambda c: pl.BlockSpec((None, tm, c), lambda bi, i: (bi, i, 0))
    full = lambda a: pl.BlockSpec(a.shape, lambda bi, i: (0,) * a.ndim)
    pos = pl.BlockSpec((tm, LANES), lambda bi, i: (i, 0))
    def deint(d, transposed):
        blk = (LANES, tm // d) if transposed else (tm // d, LANES)
        idx = (lambda bi, i: (bi, 0, 0, 0, i)) if transposed else (lambda bi, i: (bi, 0, 0, i, 0))
        dims = (LANES, s // d) if transposed else (s // d, LANES)
        return (pl.BlockSpec((None, N_SLABS, d) + blk, idx),
                jax.ShapeDtypeStruct((b, N_SLABS, d) + dims, BF16))

    qkv = [deint(d, which == 1 and d in KT_DILATIONS) for which in range(3) for d in DILATIONS]
    tail = pl.BlockSpec((None, D_ATTN, tm), lambda bi, i: (bi, 0, jnp.maximum(i - n_skip, 0)))
    out_shapes = tuple(shape for _, shape in qkv) + (
        jax.ShapeDtypeStruct((b, D_ATTN, l_tail), F32),
        jax.ShapeDtypeStruct((b, D_ATTN, l_tail), F32),
        jax.ShapeDtypeStruct((b, s, D_ATTN), BF16),
        jax.ShapeDtypeStruct((b, s, D_CONV), F32),
        jax.ShapeDtypeStruct((b, s, D_CONV), BF16),
        jax.ShapeDtypeStruct((b, s, D_SCONV), BF16),
        jax.ShapeDtypeStruct((b, s, D_SCONV), F32),
    )
    out_specs = tuple(spec for spec, _ in qkv) + (
        tail, tail, tile(D_ATTN), tile(D_CONV), tile(D_CONV), tile(D_SCONV), tile(D_SCONV))
    w_spec = _layer_spec(w_bf, layer, pipeline_mode=pl.Buffered(1))
    in_specs = [tile(D_MODEL), _layer_spec(lng, layer), w_spec, pos, pos,
                _layer_spec(qg, layer), _layer_spec(kg, layer), full(seg)]
    args = [x, lng, w_bf, cos_t, sin_t, qg, kg, seg]
    if shift is not None:
        cache, cols = shift
        depth, bd, _, l_cache = cache.shape
        assert depth * bd == grid[0] * grid[1]
        cache_spec, cols_spec = _shift_specs(depth, bd, l_cache, lambda bi, i: bi * grid[1] + i)
        in_specs += [cache_spec, cols_spec]
        args += [cache, cols]
        out_specs += (cache_spec,)
        out_shapes += (jax.ShapeDtypeStruct(cache.shape, F32),)
    return pl.pallas_call(
        functools.partial(_in_proj_prompt_kernel, tm=tm, shift=shift is not None),
        grid=grid,
        in_specs=in_specs,
        out_specs=out_specs,
        out_shape=out_shapes,
        scratch_shapes=[pltpu.VMEM((2, 1, tm, LANES), F32),
                        pltpu.VMEM((2, DILATIONS[1], tm // DILATIONS[1], LANES), F32)],
        compiler_params=pltpu.CompilerParams(dimension_semantics=("arbitrary", "arbitrary"),
                                             vmem_limit_bytes=VMEM_LIMIT),
        name="in_proj",
    )(*args)


def _in_proj_sample_kernel(x_ref, lng_ref, w_ref, cos_ref, sin_ref, qg_ref, kg_ref, seg_ref,
                           q_ref, k_ref, v_ref, zs_ref, u_ref, gza_ref, bz_ref, uc_ref):
    def emitter(out):
        def emit(s, val):
            out[:, s * LANES:(s + 1) * LANES] = val
        return emit

    _in_proj_body(x_ref, lng_ref, w_ref, cos_ref, sin_ref, qg_ref, kg_ref, seg_ref,
                  zs_ref, u_ref, gza_ref, bz_ref, uc_ref,
                  emitter(q_ref), emitter(k_ref), emitter(v_ref), q_scale=HEAD_DIM ** -0.5)


def _in_proj_sample(x2d, lng, w_bf, cos_t, sin_t, qg, kg, seg, layer):
    n = x2d.shape[0]
    args = (x2d, lng, w_bf, cos_t, sin_t, qg, kg, seg)
    full = lambda a: pl.BlockSpec(a.shape, lambda i: (0,) * a.ndim)
    per_layer = lambda a: _layer_spec(a, layer)
    widths = (D_ATTN, D_ATTN, D_ATTN, D_ATTN, D_CONV, D_CONV, D_SCONV, D_SCONV)
    out_shapes = tuple(jax.ShapeDtypeStruct((n, c), F32) for c in widths)
    return pl.pallas_call(
        _in_proj_sample_kernel,
        grid=(1,),
        in_specs=[full(x2d), per_layer(lng), per_layer(w_bf), full(cos_t), full(sin_t),
                  per_layer(qg), per_layer(kg), full(seg)],
        out_specs=tuple(full(s) for s in out_shapes),
        out_shape=out_shapes,
        compiler_params=pltpu.CompilerParams(dimension_semantics=("arbitrary",),
                                             vmem_limit_bytes=VMEM_LIMIT),
        name="in_proj_sample",
    )(*args)


ATTN_TILE = 2048
assert all(ATTN_TILE % (WIN_KEYS * d) == 0 for d in DILATIONS)
ATTN_UNROLL = 8
assert (ATTN_TILE // WIN_KEYS) % ATTN_UNROLL == 0


def _attn_kernel(*refs, tile):
    n_pat = len(DILATIONS)
    qkv_refs = [refs[3 * p:3 * p + 3] for p in range(n_pat)]
    bias_ref, o_ref, acc_ref, m_ref, l_ref = refs[3 * n_pat:]
    t = pl.program_id(2)
    blk = WIN_KEYS
    n_blocks = tile // blk
    lane = lax.broadcasted_iota(jnp.int32, (1, LANES), 1)
    head_a = lane < HEAD_DIM

    def block_softmax(qb, k_ref, v_ref, r, q0, k_channel_major):
        kk = pl.multiple_of(jnp.maximum(q0 - blk, 0), blk)
        bias = bias_ref[jnp.minimum(q0, 1)]
        if k_channel_major:
            kb = k_ref[r, :, pl.ds(kk, 2 * blk)]
            contract = (((1,), (0,)), ((), ()))
        else:
            kb = k_ref[r, pl.ds(kk, 2 * blk), :]
            contract = (((1,), (1,)), ((), ()))
        vb = v_ref[r, pl.ds(kk, 2 * blk), :]
        res, mx = [], []
        for sel in (head_a, jnp.logical_not(head_a)):
            qh = jnp.where(sel, qb, jnp.zeros_like(qb))
            s = lax.dot_general(qh, kb, contract, preferred_element_type=F32)
            s = s + bias
            m = jnp.max(s, axis=1, keepdims=True)
            p = jnp.exp2(s - m).astype(BF16)
            vh = jnp.where(sel, vb, jnp.ones_like(vb))
            res.append(jnp.dot(p, vh, preferred_element_type=F32))
            mx.append(m)
        num = jnp.where(head_a, res[0], res[1])
        den = pltpu.roll(jnp.where(head_a, res[1], res[0]), HEAD_DIM, 1)
        return num, den, jnp.where(head_a, mx[0], mx[1])

    order = sorted(range(n_pat), key=lambda p: -DILATIONS[p])
    for step, p in enumerate(order):
        d = DILATIONS[p]
        q_ref, k_ref, v_ref = qkv_refs[p]
        per_res = n_blocks // d
        assert per_res & (per_res - 1) == 0
        first, last = step == 0, step == n_pat - 1

        def body(it, carry, d=d, q_ref=q_ref, k_ref=k_ref, v_ref=v_ref, per_res=per_res,
                 first=first, last=last):
            for u in range(ATTN_UNROLL):
                i = it * ATTN_UNROLL + u
                r = lax.shift_right_logical(i, per_res.bit_length() - 1)
                jb = i & (per_res - 1)
                qb = q_ref[r, pl.ds(pl.multiple_of(jb * blk, blk), blk), :]
                num, den, mb = block_softmax(qb, k_ref, v_ref, r, t * (tile // d) + jb * blk,
                                             d in KT_DILATIONS)
                start = jb * blk * d + r
                rows = pl.ds(start, blk) if d == 1 else pl.ds(start, blk, stride=d)
                if not first:
                    m_old = m_ref[rows, :]
                    m_new = jnp.maximum(m_old, mb)
                    w_old = jnp.exp2(m_old - m_new)
                    w_blk = jnp.exp2(mb - m_new)
                    num = acc_ref[rows, :] * w_old + num * w_blk
                    den = l_ref[rows, :] * w_old + den * w_blk
                    mb = m_new
                if last:
                    o_ref[rows, :] = num / den
                else:
                    acc_ref[rows, :] = num
                    l_ref[rows, :] = den
                    m_ref[rows, :] = mb
            return carry

        lax.fori_loop(0, n_blocks // ATTN_UNROLL, body, 0)


def _band_bias():
    rel = np.arange(2 * WIN_KEYS)[None, :] - np.arange(WIN_KEYS)[:, None]
    variants = [rel - off for off in (0, WIN_KEYS)]
    return jnp.asarray(np.stack([np.where((v <= 0) & (v >= -WIN_KEYS), 0.0, NEG)
                                 for v in variants]), dtype=F32)


def _attention(qkv, b, s):
    tile = ATTN_TILE
    in_specs, args = [], []
    for d, (q, k, v) in zip(DILATIONS, qkv):
        qspec = pl.BlockSpec((None, None, d, tile // d, LANES), lambda bi, c, i: (bi, c, 0, i, 0))
        kvspec = pl.BlockSpec((None, None, d, s // d, LANES), lambda bi, c, i: (bi, c, 0, 0, 0))
        ktspec = pl.BlockSpec((None, None, d, LANES, s // d), lambda bi, c, i: (bi, c, 0, 0, 0))
        in_specs += [qspec, ktspec if d in KT_DILATIONS else kvspec, kvspec]
        args += [q, k, v]
    bias = _band_bias()
    in_specs.append(pl.BlockSpec(bias.shape, lambda bi, c, i: (0, 0, 0)))
    args.append(bias)
    return pl.pallas_call(
        functools.partial(_attn_kernel, tile=tile),
        grid=(b, N_SLABS, s // tile),
        in_specs=in_specs,
        out_specs=pl.BlockSpec((None, tile, LANES), lambda bi, c, i: (bi, i, c)),
        out_shape=jax.ShapeDtypeStruct((b, s, D_ATTN), F32),
        scratch_shapes=[pltpu.VMEM((tile, LANES), F32)] * 3,
        compiler_params=pltpu.CompilerParams(
            dimension_semantics=("arbitrary", "arbitrary", "arbitrary"),
            vmem_limit_bytes=VMEM_LIMIT),
        name="dilated_attn",
    )(*args)


A_HALO = 32
C_HALO = 8
OUT_CHUNK = 128


def _layernorm_rows(x, g, b):
    mu = jnp.mean(x, axis=-1, keepdims=True)
    xc = x - mu
    var = jnp.mean(xc * xc, axis=-1, keepdims=True)
    return xc * lax.rsqrt(var + EPS) * g + b


def _out_proj_kernel(x_ref, attn_ref, zs_ref, u_ref, uh_ref, gza_ref, bz_ref, uc_ref, uch_ref,
                     aw_ref, ab_ref, ag_ref, alb_ref, cw_ref, w_ref,
                     y_ref, ubuf, uphase, ucbuf, *, tm):
    first_tile = pl.program_id(1) == 0
    y_ref[...] = x_ref[...] + jnp.dot((attn_ref[...] * zs_ref[...]).astype(BF16),
                                      w_ref[0:D_ATTN, :], preferred_element_type=F32)

    ubuf[0:A_HALO, :] = jnp.where(first_tile, 0.0, uh_ref[...])
    ubuf[A_HALO:A_HALO + tm, :] = u_ref[...]
    n_rows = tm + A_HALO - SUBLANES
    for b in range(1, SUBLANES):
        uphase[b - 1] = ubuf[b:b + n_rows, :]
    ucbuf[0:C_HALO, :] = jnp.where(first_tile, 0.0, uch_ref[...])
    ucbuf[C_HALO:C_HALO + tm, :] = uc_ref[...]

    base = A_HALO - (CONV_WIDTH - 1)
    cbase = C_HALO - (SCONV_WIDTH - 1)
    groups = (OUT_CHUNK // SUBLANES, SUBLANES, D_CONV)
    for c in range(tm // OUT_CHUNK):
        r0 = c * OUT_CHUNK
        rows = slice(r0, r0 + OUT_CHUNK)
        ca = jnp.zeros(groups, F32)
        for j in range(CONV_WIDTH):
            b = (base + j) % SUBLANES
            a8 = base + j - b + r0
            src = (ubuf[a8:a8 + OUT_CHUNK, :] if b == 0
                   else uphase[b - 1, a8:a8 + OUT_CHUNK, :])
            ca = ca + aw_ref[j][None] * src.reshape(groups)
        ca = ca.reshape(OUT_CHUNK, D_CONV) + ab_ref[...]
        ya = _silu(_layernorm_rows(ca, ag_ref[...], alb_ref[...])) * gza_ref[rows, :]
        cc = jnp.zeros((OUT_CHUNK, D_SCONV), F32)
        for j in range(SCONV_WIDTH):
            cc = cc + cw_ref[j:j + 1, :] * ucbuf[cbase + j + r0:cbase + j + r0 + OUT_CHUNK, :]
        yc = bz_ref[rows, :] * cc
        y_ref[rows, :] += (
            jnp.dot(ya.astype(BF16), w_ref[D_ATTN:D_ATTN + D_CONV, :], preferred_element_type=F32)
            + jnp.dot(yc.astype(BF16), w_ref[D_ATTN + D_CONV:, :], preferred_element_type=F32))


def _out_proj(x, attn, zs, u, gza, bz, uc, aw, ab, ag, alb, cw, w_bf, layer, tm):
    b, s, _ = x.shape
    tile = lambda c: pl.BlockSpec((None, tm, c), lambda bi, i: (bi, i, 0))
    halo = lambda h, c: pl.BlockSpec(
        (None, h, c), lambda bi, i: (bi, jnp.maximum(i * (tm // h) - 1, 0), 0))
    return pl.pallas_call(
        functools.partial(_out_proj_kernel, tm=tm),
        grid=(b, s // tm),
        in_specs=[tile(D_MODEL), tile(D_ATTN), tile(D_ATTN),
                  tile(D_CONV), halo(A_HALO, D_CONV), tile(D_CONV), tile(D_SCONV),
                  tile(D_SCONV), halo(C_HALO, D_SCONV)]
                 + [_layer_spec(a, layer) for a in (aw, ab, ag, alb, cw, w_bf)],
        out_specs=tile(D_MODEL),
        out_shape=jax.ShapeDtypeStruct((b, s, D_MODEL), F32),
        scratch_shapes=[pltpu.VMEM((A_HALO + tm, D_CONV), F32),
                        pltpu.VMEM((SUBLANES - 1, A_HALO + tm - SUBLANES, D_CONV), F32),
                        pltpu.VMEM((C_HALO + tm, D_SCONV), F32)],
        compiler_params=pltpu.CompilerParams(dimension_semantics=("arbitrary", "arbitrary"),
                                             vmem_limit_bytes=VMEM_LIMIT),
        name="out_proj",
    )(x, attn, zs, u, u, gza, bz, uc, uc, aw, ab, ag, alb, cw, w_bf)


def _sample_attn_kernel(q_ref, kn_ref, vn_ref, kt_ref, vt_ref, attn_ref):
    l_cache = kt_ref.shape[1]
    kn = kn_ref[...]
    vn = vn_ref[...]
    head_of_lane = lax.broadcasted_iota(jnp.int32, (N_HEADS, D_ATTN), 1) // HEAD_DIM
    own = head_of_lane == lax.broadcasted_iota(jnp.int32, (N_HEADS, D_ATTN), 0)
    qblk = jnp.where(own, q_ref[...], 0.0)
    s = jnp.dot(qblk.astype(BF16), kt_ref[...].astype(BF16), preferred_element_type=F32)
    s_self = jnp.sum(qblk * kn, axis=1, keepdims=True)

    dist = l_cache - lax.broadcasted_iota(jnp.int32, (N_HEADS, l_cache), 1)
    cnt = jnp.zeros((N_HEADS, l_cache), F32)
    for w, d in zip(WINDOWS, DILATIONS):
        cnt = cnt + jnp.where((dist <= w) & ((dist & (d - 1)) == 0), 1.0, 0.0)
    s = jnp.where(cnt > 0, s, NEG)
    m = jnp.maximum(jnp.max(s, axis=1, keepdims=True), s_self)
    e = (cnt * jnp.exp(s - m)).astype(BF16)
    e_self = len(WINDOWS) * jnp.exp(s_self - m)
    den = jnp.sum(e.astype(F32), axis=1, keepdims=True) + e_self
    pv = lax.dot_general(e, vt_ref[...].astype(BF16), (((1,), (1,)), ((), ())),
                         preferred_element_type=F32)
    per_lane = lambda a: jnp.sum(jnp.where(own, a, 0.0), axis=0, keepdims=True)
    num = per_lane(pv) + per_lane(jnp.broadcast_to(e_self, own.shape)) * vn
    attn_ref[...] = num / per_lane(jnp.broadcast_to(den, own.shape))


def _sample_attn(q, kf, vf, cache_kt, cache_vt, layer):
    _, bd, _, l_cache = cache_kt.shape
    assert l_cache >= WINDOWS[-1]
    row = pl.BlockSpec((None, 1, D_ATTN), lambda b: (b, 0, 0))
    cache = pl.BlockSpec((None, None, D_ATTN, l_cache), lambda b: (layer, b, 0, 0))
    r3 = lambda a: a.reshape(bd, 1, D_ATTN)
    attn = pl.pallas_call(
        _sample_attn_kernel,
        grid=(bd,),
        in_specs=[row, row, row, cache, cache],
        out_specs=row,
        out_shape=jax.ShapeDtypeStruct((bd, 1, D_ATTN), F32),
        compiler_params=pltpu.CompilerParams(dimension_semantics=("arbitrary",),
                                             vmem_limit_bytes=VMEM_LIMIT),
        name="sample_attn",
    )(r3(q), r3(kf), r3(vf), cache_kt, cache_vt)
    return attn.reshape(bd, D_ATTN)


def _sample_out_kernel(x_ref, attn_ref, zs_ref, u_ref, gza_ref, bz_ref, uc_ref, sa_ref, sc_ref,
                       aw_ref, ab_ref, ag_ref, alb_ref, cw_ref, w_ref, y_ref, na_ref, nc_ref):
    u = u_ref[...]
    uc = uc_ref[...]
    na = CONV_WIDTH - 1
    nc = SCONV_WIDTH - 1
    ca = ab_ref[...] + aw_ref[na:na + 1, :] * u
    for j in range(na):
        ca = ca + aw_ref[j:j + 1, :] * sa_ref[:, j * D_CONV:(j + 1) * D_CONV]
    ya = _silu(_layernorm_rows(ca, ag_ref[...], alb_ref[...])) * gza_ref[...]
    cc = cw_ref[nc:nc + 1, :] * uc
    for j in range(nc):
        cc = cc + cw_ref[j:j + 1, :] * sc_ref[:, j * D_SCONV:(j + 1) * D_SCONV]
    yc = bz_ref[...] * cc
    acc = jnp.dot((attn_ref[...] * zs_ref[...]).astype(BF16), w_ref[0:D_ATTN, :],
                  preferred_element_type=F32)
    acc = acc + jnp.dot(ya.astype(BF16), w_ref[D_ATTN:D_ATTN + D_CONV, :],
                        preferred_element_type=F32)
    acc = acc + jnp.dot(yc.astype(BF16), w_ref[D_ATTN + D_CONV:, :],
                        preferred_element_type=F32)
    y_ref[...] = x_ref[...] + acc
    na_ref[:, 0:(na - 1) * D_CONV] = sa_ref[:, D_CONV:na * D_CONV]
    na_ref[:, (na - 1) * D_CONV:] = u
    nc_ref[:, 0:(nc - 1) * D_SCONV] = sc_ref[:, D_SCONV:nc * D_SCONV]
    nc_ref[:, (nc - 1) * D_SCONV:] = uc


def _sample_out(x, attn, zs, u, gza, bz, uc, sa, sc, aw, ab, ag, alb, cw, w_bf, layer):
    depth, bd = sa.shape[:2]
    sa2 = sa.reshape(depth, bd, -1)
    sc2 = sc.reshape(depth, bd, -1)
    per_token = (x, attn, zs, u, gza, bz, uc)
    per_layer = (sa2, sc2, aw, ab, ag, alb, cw, w_bf)
    full = lambda a: pl.BlockSpec(a.shape, lambda i: (0,) * a.ndim)
    y, na, nc = pl.pallas_call(
        _sample_out_kernel,
        grid=(1,),
        in_specs=[full(a) for a in per_token] + [_layer_spec(a, layer) for a in per_layer],
        out_specs=(full(x), pl.BlockSpec(sa2.shape[1:], lambda i: (0, 0)),
                   pl.BlockSpec(sc2.shape[1:], lambda i: (0, 0))),
        out_shape=(jax.ShapeDtypeStruct(x.shape, F32),
                   jax.ShapeDtypeStruct(sa2.shape[1:], F32),
                   jax.ShapeDtypeStruct(sc2.shape[1:], F32)),
        compiler_params=pltpu.CompilerParams(dimension_semantics=("arbitrary",),
                                             vmem_limit_bytes=VMEM_LIMIT),
        name="sample_out",
    )(*per_token, *per_layer)
    return y, na.reshape(sa.shape[1:]), nc.reshape(sc.shape[1:])


def _rope_tables(pos):
    half = HEAD_DIM // 2
    inv = ROPE_THETA ** (-jnp.arange(half, dtype=F32) / half)
    ang = pos.astype(F32)[:, None] * inv[None, :]
    cos, sin = jnp.cos(ang), jnp.sin(ang)
    reps = LANES // HEAD_DIM
    cos_t = jnp.tile(jnp.concatenate([cos, cos], axis=-1), (1, reps))
    sin_t = jnp.tile(jnp.concatenate([-sin, sin], axis=-1), (1, reps))
    return cos_t, sin_t


def _segment_mean_matrix():
    seg = np.kron(np.eye(LANES // HEAD_DIM), np.ones((HEAD_DIM, HEAD_DIM))) / HEAD_DIM
    return jnp.asarray(seg, dtype=BF16)


def _to_channel_major(cache):
    lead = cache.shape[:-3]
    n = len(lead)
    perm = tuple(range(n)) + (n + 1, n + 2, n)
    return jnp.transpose(cache, perm).reshape(*lead, D_ATTN, cache.shape[-3])


def _from_channel_major(cache_t):
    lead = cache_t.shape[:-2]
    n = len(lead)
    perm = tuple(range(n)) + (n + 2, n, n + 1)
    return jnp.transpose(cache_t.reshape(*lead, N_HEADS, HEAD_DIM, cache_t.shape[-1]), perm)


def kernel(x_prompt, x_sample, cache_k, cache_v, state_conv_a, state_conv_c, ln_g, w_in,
           q_norm_g, k_norm_g, a_conv_w, a_conv_b, a_ln_g, a_ln_b, c_conv_w, w_out):
    bp, sp, _ = x_prompt.shape
    bd, ts, _ = x_sample.shape
    assert ts == 1
    depth = w_in.shape[0]
    l_prompt = min(WINDOWS[-1], sp)

    cos_p, sin_p = _rope_tables(jnp.arange(sp, dtype=jnp.int32))
    cos_s, sin_s = _rope_tables(jnp.full((bd,), PAST_LEN, dtype=jnp.int32))
    seg = _segment_mean_matrix()
    cache_kt = _to_channel_major(cache_k)
    cache_vt = _to_channel_major(cache_v)

    rows = lambda a: a.reshape(depth, 1, -1)
    w_in_bf = w_in.astype(BF16)
    w_out_bf = w_out.astype(BF16)
    lng = rows(ln_g)
    qg = rows(jnp.tile(q_norm_g, (1, LANES // HEAD_DIM)))
    kg = rows(jnp.tile(k_norm_g, (1, LANES // HEAD_DIM)))
    conv_w = (a_conv_w, rows(a_conv_b), rows(a_ln_g), rows(a_ln_b), c_conv_w)
    aw8 = jnp.broadcast_to(a_conv_w[:, :, None, :], (depth, CONV_WIDTH, SUBLANES, D_CONV))

    hs = x_sample.reshape(bd, D_MODEL)
    sk, sv, sa, sc = [], [], [], []
    for l in range(depth):
        q, kf, vf, zs, u, gza, bz, uc = _in_proj_sample(hs, lng, w_in_bf, cos_s, sin_s, qg, kg,
                                                        seg, l)
        attn = _sample_attn(q, kf, vf, cache_kt, cache_vt, l)
        hs, na, nc = _sample_out(hs, attn, zs, u, gza, bz, uc, state_conv_a, state_conv_c,
                                 *conv_w, w_out_bf, l)
        sk.append(kf)
        sv.append(vf)
        sa.append(na)
        sc.append(nc)
    new_cols = lambda new_rows: jnp.transpose(jnp.stack(new_rows), (0, 2, 1))

    assert depth >= 2
    shift_jobs = {depth - 2: (cache_kt, new_cols(sk)), depth - 1: (cache_vt, new_cols(sv))}
    shifted = {}

    hp = x_prompt
    pk, pv, pa, pc = [], [], [], []
    for l in range(depth):
        outs = _in_proj_prompt(hp, lng, w_in_bf, cos_p, sin_p, qg, kg, seg, l, PROMPT_TM_IN,
                               l_prompt, shift=shift_jobs.get(l))
        if l in shift_jobs:
            outs, shifted[l] = outs[:-1], outs[-1]
        n_pat = len(DILATIONS)
        q_d, k_d, v_d = outs[0:n_pat], outs[n_pat:2 * n_pat], outs[2 * n_pat:3 * n_pat]
        kt, vt, zs, u, gza, bz, uc = outs[3 * n_pat:]
        attn = _attention(list(zip(q_d, k_d, v_d)), bp, sp)
        hp = _out_proj(hp, attn, zs, u, gza, bz, uc, aw8, *conv_w[1:], w_out_bf, l,
                       PROMPT_TM_OUT)
        pk.append(kt)
        pv.append(vt)
        pa.append(u[:, sp - (CONV_WIDTH - 1):])
        pc.append(uc[:, sp - (SCONV_WIDTH - 1):])

    return (hp, hs.reshape(bd, ts, D_MODEL),
            _from_channel_major(jnp.stack(pk)), _from_channel_major(jnp.stack(pv)),
            jnp.stack(pa), jnp.stack(pc),
            _from_channel_major(shifted[depth - 2]), _from_channel_major(shifted[depth - 1]),
            jnp.stack(sa), jnp.stack(sc))
```

```python
import functools

import numpy as np
import jax
import jax.numpy as jnp
from jax import lax
from jax.experimental import pallas as pl
from jax.experimental.pallas import tpu as pltpu

D_MODEL = 1024
N_HEADS = 8
HEAD_DIM = 64
D_ATTN = N_HEADS * HEAD_DIM
D_CONV = 256
CONV_WIDTH = 31
D_SCONV = 256
SCONV_WIDTH = 3
WINDOWS = (128, 512, 2048)
DILATIONS = (1, 4, 16)
WIN_KEYS = 128
assert all(w // d == WIN_KEYS for w, d in zip(WINDOWS, DILATIONS))
assert all(d & (d - 1) == 0 for d in DILATIONS)
assert len(DILATIONS) == 3 and DILATIONS[0] == 1 and DILATIONS[2] % DILATIONS[1] == 0
PAST_LEN = 16384
ROPE_THETA = 10000.0
EPS = 1e-6
NEG = -1e30
LOG2_E = 1.4426950408889634

C_Q, C_K, C_V, C_Z = 0, 512, 1024, 1536
C_AVAL, C_AGATE, C_ZA = 2048, 2304, 2560
C_CB, C_CC, C_CH, C_ZC = 2816, 3072, 3328, 3584
D_IN = 3840

LANES = 128
SUBLANES = 8
N_SLABS = D_ATTN // LANES
VMEM_LIMIT = 56 * 1024 * 1024

PROMPT_TM_IN = 512
PROMPT_TM_OUT = 1024
KT_DILATIONS = tuple(d for d in DILATIONS if (PROMPT_TM_IN // d) % LANES == 0)

F32 = jnp.float32
BF16 = jnp.bfloat16


def _layer_spec(a, layer, **kwargs):
    return pl.BlockSpec((None,) + a.shape[1:], lambda *_: (layer,) + (0,) * (a.ndim - 1), **kwargs)


def _sigmoid(x):
    return 1.0 / (1.0 + jnp.exp(-x))


def _silu(x):
    return x * _sigmoid(x)


SHIFT_ROWS = 32


def _shift_cache_block(c_ref, nt_ref, n_ref, b):
    n_rows, l_cache = c_ref.shape
    bd = nt_ref.shape[1]
    mine = lax.broadcasted_iota(jnp.int32, (SHIFT_ROWS, bd), 1) == b
    last_lane = lax.broadcasted_iota(jnp.int32, (SHIFT_ROWS, LANES), 1) == LANES - 1
    for c in range(n_rows // SHIFT_ROWS):
        rows = slice(c * SHIFT_ROWS, (c + 1) * SHIFT_ROWS)
        new_col = jnp.sum(jnp.where(mine, nt_ref[rows, :], 0.0), axis=1, keepdims=True)
        rolled = pltpu.roll(c_ref[rows, :], l_cache - 1, 1)
        n_ref[rows, 0:l_cache - LANES] = rolled[:, 0:l_cache - LANES]
        n_ref[rows, l_cache - LANES:] = jnp.where(last_lane, new_col, rolled[:, l_cache - LANES:])


def _shift_specs(depth, bd, l_cache, step_of):
    cache = pl.BlockSpec((None, None, D_ATTN, l_cache),
                         lambda *g: (step_of(*g) // bd, step_of(*g) % bd, 0, 0))
    cols = pl.BlockSpec((None, D_ATTN, bd), lambda *g: (step_of(*g) // bd, 0, 0))
    return cache, cols


def _in_proj_body(x_ref, lng_ref, w_ref, cos_ref, sin_ref, qg_ref, kg_ref, seg_ref,
                  zs_ref, u_ref, gza_ref, bz_ref, uc_ref, emit_q, emit_k, emit_v, q_scale):
    x = x_ref[...]
    ms = jnp.mean(x * x, axis=-1, keepdims=True)
    xn = (x * lax.rsqrt(ms + EPS) * lng_ref[...]).astype(BF16)

    def proj(c0, c1):
        return jnp.dot(xn, w_ref[:, c0:c1], preferred_element_type=F32)

    cos = cos_ref[...]
    sin = sin_ref[...]
    tm = x.shape[0]
    first_half = (lax.broadcasted_iota(jnp.int32, (tm, LANES), 1) & (HEAD_DIM // 2)) == 0

    def norm_rope(p, g_ref, scale, emit):
        for s in range(N_SLABS):
            ps = p[:, s * LANES:(s + 1) * LANES]
            ss = jnp.dot((ps * ps).astype(BF16), seg_ref[...], preferred_element_type=F32)
            pn = ps * lax.rsqrt(ss + EPS) * g_ref[...]
            partner = jnp.where(first_half,
                                pltpu.roll(pn, LANES - HEAD_DIM // 2, 1),
                                pltpu.roll(pn, HEAD_DIM // 2, 1))
            emit(s, (pn * cos + partner * sin) * scale)

    norm_rope(proj(C_Q, C_K), qg_ref, q_scale, emit_q)
    norm_rope(proj(C_K, C_V), kg_ref, 1.0, emit_k)
    v = proj(C_V, C_Z)
    for s in range(N_SLABS):
        emit_v(s, v[:, s * LANES:(s + 1) * LANES])
    zs_ref[...] = _silu(proj(C_Z, C_AVAL)).astype(zs_ref.dtype)
    ag = proj(C_AVAL, C_ZA)
    u_ref[...] = ag[:, :D_CONV] * _sigmoid(ag[:, D_CONV:])
    zb = proj(C_ZA, C_CC)
    gza_ref[...] = _silu(zb[:, :D_CONV]).astype(gza_ref.dtype)
    zc = proj(C_ZC, D_IN)
    bz_ref[...] = (zb[:, D_CONV:] * _silu(zc)).astype(bz_ref.dtype)
    ch = proj(C_CC, C_ZC)
    uc_ref[...] = ch[:, :D_SCONV] * ch[:, D_SCONV:]


def _in_proj_prompt_kernel(x_ref, lng_ref, w_ref, cos_ref, sin_ref, qg_ref, kg_ref, seg_ref,
                           *rest, tm, shift):
    if shift:
        (c_ref, nt_ref, q1_ref, q4_ref, q16_ref, k1_ref, k4_ref, k16_ref, v1_ref, v4_ref, v16_ref,
         kt_ref, vt_ref, zs_ref, u_ref, gza_ref, bz_ref, uc_ref, n_ref,
         stage_ref, stage_next_ref) = rest
        grid_step = pl.program_id(0) * pl.num_programs(1) + pl.program_id(1)
        _shift_cache_block(c_ref, nt_ref, n_ref, grid_step % nt_ref.shape[1])
    else:
        (q1_ref, q4_ref, q16_ref, k1_ref, k4_ref, k16_ref, v1_ref, v4_ref, v16_ref,
         kt_ref, vt_ref, zs_ref, u_ref, gza_ref, bz_ref, uc_ref,
         stage_ref, stage_next_ref) = rest

    def emitter(which, outs, t_ref, transposed=()):
        def emit(s, val):
            slot = (which * N_SLABS + s) % 2
            levels = {1: stage_ref.at[slot]}
            levels[1][0] = val
            val_t = val.T if t_ref is not None else None
            for li, (d, out) in enumerate(zip(DILATIONS, outs)):
                prev = DILATIONS[li - 1] if li else 1
                keep = li + 1 < len(DILATIONS)
                for r in range(d):
                    if d == 1:
                        part = val
                    else:
                        part = levels[prev][r % prev, pl.ds(r // prev, tm // d, stride=d // prev), :]
                        if keep:
                            stage_next_ref[slot, r] = part
                    if d in transposed:
                        part = val_t if d == 1 else part.T
                    out[s, r] = part.astype(BF16)
                if d != 1 and keep:
                    levels[d] = stage_next_ref.at[slot]
            if t_ref is not None:
                t_ref[s * LANES:(s + 1) * LANES, :] = val_t
        return emit

    _in_proj_body(x_ref, lng_ref, w_ref, cos_ref, sin_ref, qg_ref, kg_ref, seg_ref,
                  zs_ref, u_ref, gza_ref, bz_ref, uc_ref,
                  emitter(0, (q1_ref, q4_ref, q16_ref), None),
                  emitter(1, (k1_ref, k4_ref, k16_ref), kt_ref, transposed=KT_DILATIONS),
                  emitter(2, (v1_ref, v4_ref, v16_ref), vt_ref),
                  q_scale=HEAD_DIM ** -0.5 * LOG2_E)


def _in_proj_prompt(x, lng, w_bf, cos_t, sin_t, qg, kg, seg, layer, tm, l_tail, shift=None):
    b, s, _ = x.shape
    grid = (b, s // tm)
    n_skip = (s - l_tail) // tm
    tile = lambda c: pl.BlockSpec((None, tm, c), lambda bi, i: (bi, i, 0))
    full = lambda a: pl.BlockSpec(a.shape, lambda bi, i: (0,) * a.ndim)
    pos = pl.BlockSpec((tm, LANES), lambda bi, i: (i, 0))
    def deint(d, transposed):
        blk = (LANES, tm // d) if transposed else (tm // d, LANES)
        idx = (lambda bi, i: (bi, 0, 0, 0, i)) if transposed else (lambda bi, i: (bi, 0, 0, i, 0))
        dims = (LANES, s // d) if transposed else (s // d, LANES)
        return (pl.BlockSpec((None, N_SLABS, d) + blk, idx),
                jax.ShapeDtypeStruct((b, N_SLABS, d) + dims, BF16))

    qkv = [deint(d, which == 1 and d in KT_DILATIONS) for which in range(3) for d in DILATIONS]
    tail = pl.BlockSpec((None, D_ATTN, tm), lambda bi, i: (bi, 0, jnp.maximum(i - n_skip, 0)))
    out_shapes = tuple(shape for _, shape in qkv) + (
        jax.ShapeDtypeStruct((b, D_ATTN, l_tail), F32),
        jax.ShapeDtypeStruct((b, D_ATTN, l_tail), F32),
        jax.ShapeDtypeStruct((b, s, D_ATTN), BF16),
        jax.ShapeDtypeStruct((b, s, D_CONV), F32),
        jax.ShapeDtypeStruct((b, s, D_CONV), BF16),
        jax.ShapeDtypeStruct((b, s, D_SCONV), BF16),
        jax.ShapeDtypeStruct((b, s, D_SCONV), F32),
    )
    out_specs = tuple(spec for spec, _ in qkv) + (
        tail, tail, tile(D_ATTN), tile(D_CONV), tile(D_CONV), tile(D_SCONV), tile(D_SCONV))
    w_spec = _layer_spec(w_bf, layer, pipeline_mode=pl.Buffered(1))
    in_specs = [tile(D_MODEL), _layer_spec(lng, layer), w_spec, pos, pos,
                _layer_spec(qg, layer), _layer_spec(kg, layer), full(seg)]
    args = [x, lng, w_bf, cos_t, sin_t, qg, kg, seg]
    if shift is not None:
        cache, cols = shift
        depth, bd, _, l_cache = cache.shape
        assert depth * bd == grid[0] * grid[1]
        cache_spec, cols_spec = _shift_specs(depth, bd, l_cache, lambda bi, i: bi * grid[1] + i)
        in_specs += [cache_spec, cols_spec]
        args += [cache, cols]
        out_specs += (cache_spec,)
        out_shapes += (jax.ShapeDtypeStruct(cache.shape, F32),)
    return pl.pallas_call(
        functools.partial(_in_proj_prompt_kernel, tm=tm, shift=shift is not None),
        grid=grid,
        in_specs=in_specs,
        out_specs=out_specs,
        out_shape=out_shapes,
        scratch_shapes=[pltpu.VMEM((2, 1, tm, LANES), F32),
                        pltpu.VMEM((2, DILATIONS[1], tm // DILATIONS[1], LANES), F32)],
        compiler_params=pltpu.CompilerParams(dimension_semantics=("arbitrary", "arbitrary"),
                                             vmem_limit_bytes=VMEM_LIMIT),
        name="in_proj",
    )(*args)


def _in_proj_sample_kernel(x_ref, lng_ref, w_ref, cos_ref, sin_ref, qg_ref, kg_ref, seg_ref,
                           q_ref, k_ref, v_ref, zs_ref, u_ref, gza_ref, bz_ref, uc_ref):
    def emitter(out):
        def emit(s, val):
            out[:, s * LANES:(s + 1) * LANES] = val
        return emit

    _in_proj_body(x_ref, lng_ref, w_ref, cos_ref, sin_ref, qg_ref, kg_ref, seg_ref,
                  zs_ref, u_ref, gza_ref, bz_ref, uc_ref,
                  emitter(q_ref), emitter(k_ref), emitter(v_ref), q_scale=HEAD_DIM ** -0.5)


def _in_proj_sample(x2d, lng, w_bf, cos_t, sin_t, qg, kg, seg, layer):
    n = x2d.shape[0]
    args = (x2d, lng, w_bf, cos_t, sin_t, qg, kg, seg)
    full = lambda a: pl.BlockSpec(a.shape, lambda i: (0,) * a.ndim)
    per_layer = lambda a: _layer_spec(a, layer)
    widths = (D_ATTN, D_ATTN, D_ATTN, D_ATTN, D_CONV, D_CONV, D_SCONV, D_SCONV)
    out_shapes = tuple(jax.ShapeDtypeStruct((n, c), F32) for c in widths)
    return pl.pallas_call(
        _in_proj_sample_kernel,
        grid=(1,),
        in_specs=[full(x2d), per_layer(lng), per_layer(w_bf), full(cos_t), full(sin_t),
                  per_layer(qg), per_layer(kg), full(seg)],
        out_specs=tuple(full(s) for s in out_shapes),
        out_shape=out_shapes,
        compiler_params=pltpu.CompilerParams(dimension_semantics=("arbitrary",),
                                             vmem_limit_bytes=VMEM_LIMIT),
        name="in_proj_sample",
    )(*args)


ATTN_TILE = 2048
assert all(ATTN_TILE % (WIN_KEYS * d) == 0 for d in DILATIONS)
ATTN_UNROLL = 8
assert (ATTN_TILE // WIN_KEYS) % ATTN_UNROLL == 0


def _attn_kernel(*refs, tile):
    n_pat = len(DILATIONS)
    qkv_refs = [refs[3 * p:3 * p + 3] for p in range(n_pat)]
    bias_ref, o_ref, acc_ref, m_ref, l_ref = refs[3 * n_pat:]
    t = pl.program_id(2)
    blk = WIN_KEYS
    n_blocks = tile // blk
    lane = lax.broadcasted_iota(jnp.int32, (1, LANES), 1)
    head_a = lane < HEAD_DIM

    def block_softmax(qb, k_ref, v_ref, r, q0, k_channel_major):
        kk = pl.multiple_of(jnp.maximum(q0 - blk, 0), blk)
        bias = bias_ref[jnp.minimum(q0, 1)]
        if k_channel_major:
            kb = k_ref[r, :, pl.ds(kk, 2 * blk)]
            contract = (((1,), (0,)), ((), ()))
        else:
            kb = k_ref[r, pl.ds(kk, 2 * blk), :]
            contract = (((1,), (1,)), ((), ()))
        vb = v_ref[r, pl.ds(kk, 2 * blk), :]
        res, mx = [], []
        for sel in (head_a, jnp.logical_not(head_a)):
            qh = jnp.where(sel, qb, jnp.zeros_like(qb))
            s = lax.dot_general(qh, kb, contract, preferred_element_type=F32)
            s = s + bias
            m = jnp.max(s, axis=1, keepdims=True)
            p = jnp.exp2(s - m).astype(BF16)
            vh = jnp.where(sel, vb, jnp.ones_like(vb))
            res.append(jnp.dot(p, vh, preferred_element_type=F32))
            mx.append(m)
        num = jnp.where(head_a, res[0], res[1])
        den = pltpu.roll(jnp.where(head_a, res[1], res[0]), HEAD_DIM, 1)
        return num, den, jnp.where(head_a, mx[0], mx[1])

    order = sorted(range(n_pat), key=lambda p: -DILATIONS[p])
    for step, p in enumerate(order):
        d = DILATIONS[p]
        q_ref, k_ref, v_ref = qkv_refs[p]
        per_res = n_blocks // d
        assert per_res & (per_res - 1) == 0
        first, last = step == 0, step == n_pat - 1

        def body(it, carry, d=d, q_ref=q_ref, k_ref=k_ref, v_ref=v_ref, per_res=per_res,
                 first=first, last=last):
            for u in range(ATTN_UNROLL):
                i = it * ATTN_UNROLL + u
                r = lax.shift_right_logical(i, per_res.bit_length() - 1)
                jb = i & (per_res - 1)
                qb = q_ref[r, pl.ds(pl.multiple_of(jb * blk, blk), blk), :]
                num, den, mb = block_softmax(qb, k_ref, v_ref, r, t * (tile // d) + jb * blk,
                                             d in KT_DILATIONS)
                start = jb * blk * d + r
                rows = pl.ds(start, blk) if d == 1 else pl.ds(start, blk, stride=d)
                if not first:
                    m_old = m_ref[rows, :]
                    m_new = jnp.maximum(m_old, mb)
                    w_old = jnp.exp2(m_old - m_new)
                    w_blk = jnp.exp2(mb - m_new)
                    num = acc_ref[rows, :] * w_old + num * w_blk
                    den = l_ref[rows, :] * w_old + den * w_blk
                    mb = m_new
                if last:
                    o_ref[rows, :] = num / den
                else:
                    acc_ref[rows, :] = num
                    l_ref[rows, :] = den
                    m_ref[rows, :] = mb
            return carry

        lax.fori_loop(0, n_blocks // ATTN_UNROLL, body, 0)


def _band_bias():
    rel = np.arange(2 * WIN_KEYS)[None, :] - np.arange(WIN_KEYS)[:, None]
    variants = [rel - off for off in (0, WIN_KEYS)]
    return jnp.asarray(np.stack([np.where((v <= 0) & (v >= -WIN_KEYS), 0.0, NEG)
                                 for v in variants]), dtype=F32)


def _attention(qkv, b, s):
    tile = ATTN_TILE
    in_specs, args = [], []
    for d, (q, k, v) in zip(DILATIONS, qkv):
        qspec = pl.BlockSpec((None, None, d, tile // d, LANES), lambda bi, c, i: (bi, c, 0, i, 0))
        kvspec = pl.BlockSpec((None, None, d, s // d, LANES), lambda bi, c, i: (bi, c, 0, 0, 0))
        ktspec = pl.BlockSpec((None, None, d, LANES, s // d), lambda bi, c, i: (bi, c, 0, 0, 0))
        in_specs += [qspec, ktspec if d in KT_DILATIONS else kvspec, kvspec]
        args += [q, k, v]
    bias = _band_bias()
    in_specs.append(pl.BlockSpec(bias.shape, lambda bi, c, i: (0, 0, 0)))
    args.append(bias)
    return pl.pallas_call(
        functools.partial(_attn_kernel, tile=tile),
        grid=(b, N_SLABS, s // tile),
        in_specs=in_specs,
        out_specs=pl.BlockSpec((None, tile, LANES), lambda bi, c, i: (bi, i, c)),
        out_shape=jax.ShapeDtypeStruct((b, s, D_ATTN), F32),
        scratch_shapes=[pltpu.VMEM((tile, LANES), F32)] * 3,
        compiler_params=pltpu.CompilerParams(
            dimension_semantics=("arbitrary", "arbitrary", "arbitrary"),
            vmem_limit_bytes=VMEM_LIMIT),
        name="dilated_attn",
    )(*args)


A_HALO = 32
C_HALO = 8
OUT_CHUNK = 128


def _layernorm_rows(x, g, b):
    mu = jnp.mean(x, axis=-1, keepdims=True)
    xc = x - mu
    var = jnp.mean(xc * xc, axis=-1, keepdims=True)
    return xc * lax.rsqrt(var + EPS) * g + b


def _out_proj_kernel(x_ref, attn_ref, zs_ref, u_ref, uh_ref, gza_ref, bz_ref, uc_ref, uch_ref,
                     aw_ref, ab_ref, ag_ref, alb_ref, cw_ref, w_ref,
                     y_ref, ubuf, uphase, ucbuf, *, tm):
    first_tile = pl.program_id(1) == 0
    y_ref[...] = x_ref[...] + jnp.dot((attn_ref[...] * zs_ref[...]).astype(BF16),
                                      w_ref[0:D_ATTN, :], preferred_element_type=F32)

    ubuf[0:A_HALO, :] = jnp.where(first_tile, 0.0, uh_ref[...])
    ubuf[A_HALO:A_HALO + tm, :] = u_ref[...]
    n_rows = tm + A_HALO - SUBLANES
    for b in range(1, SUBLANES):
        uphase[b - 1] = ubuf[b:b + n_rows, :]
    ucbuf[0:C_HALO, :] = jnp.where(first_tile, 0.0, uch_ref[...])
    ucbuf[C_HALO:C_HALO + tm, :] = uc_ref[...]

    base = A_HALO - (CONV_WIDTH - 1)
    cbase = C_HALO - (SCONV_WIDTH - 1)
    groups = (OUT_CHUNK // SUBLANES, SUBLANES, D_CONV)
    for c in range(tm // OUT_CHUNK):
        r0 = c * OUT_CHUNK
        rows = slice(r0, r0 + OUT_CHUNK)
        ca = jnp.zeros(groups, F32)
        for j in range(CONV_WIDTH):
            b = (base + j) % SUBLANES
            a8 = base + j - b + r0
            src = (ubuf[a8:a8 + OUT_CHUNK, :] if b == 0
                   else uphase[b - 1, a8:a8 + OUT_CHUNK, :])
            ca = ca + aw_ref[j][None] * src.reshape(groups)
        ca = ca.reshape(OUT_CHUNK, D_CONV) + ab_ref[...]
        ya = _silu(_layernorm_rows(ca, ag_ref[...], alb_ref[...])) * gza_ref[rows, :]
        cc = jnp.zeros((OUT_CHUNK, D_SCONV), F32)
        for j in range(SCONV_WIDTH):
            cc = cc + cw_ref[j:j + 1, :] * ucbuf[cbase + j + r0:cbase + j + r0 + OUT_CHUNK, :]
        yc = bz_ref[rows, :] * cc
        y_ref[rows, :] += (
            jnp.dot(ya.astype(BF16), w_ref[D_ATTN:D_ATTN + D_CONV, :], preferred_element_type=F32)
            + jnp.dot(yc.astype(BF16), w_ref[D_ATTN + D_CONV:, :], preferred_element_type=F32))


def _out_proj(x, attn, zs, u, gza, bz, uc, aw, ab, ag, alb, cw, w_bf, layer, tm):
    b, s, _ = x.shape
    tile = lambda c: pl.BlockSpec((None, tm, c), lambda bi, i: (bi, i, 0))
    halo = lambda h, c: pl.BlockSpec(
        (None, h, c), lambda bi, i: (bi, jnp.maximum(i * (tm // h) - 1, 0), 0))
    return pl.pallas_call(
        functools.partial(_out_proj_kernel, tm=tm),
        grid=(b, s // tm),
        in_specs=[tile(D_MODEL), tile(D_ATTN), tile(D_ATTN),
                  tile(D_CONV), halo(A_HALO, D_CONV), tile(D_CONV), tile(D_SCONV),
                  tile(D_SCONV), halo(C_HALO, D_SCONV)]
                 + [_layer_spec(a, layer) for a in (aw, ab, ag, alb, cw, w_bf)],
        out_specs=tile(D_MODEL),
        out_shape=jax.ShapeDtypeStruct((b, s, D_MODEL), F32),
        scratch_shapes=[pltpu.VMEM((A_HALO + tm, D_CONV), F32),
                        pltpu.VMEM((SUBLANES - 1, A_HALO + tm - SUBLANES, D_CONV), F32),
                        pltpu.VMEM((C_HALO + tm, D_SCONV), F32)],
        compiler_params=pltpu.CompilerParams(dimension_semantics=("arbitrary", "arbitrary"),
                                             vmem_limit_bytes=VMEM_LIMIT),
        name="out_proj",
    )(x, attn, zs, u, u, gza, bz, uc, uc, aw, ab, ag, alb, cw, w_bf)


def _sample_attn_kernel(q_ref, kn_ref, vn_ref, kt_ref, vt_ref, attn_ref):
    l_cache = kt_ref.shape[1]
    kn = kn_ref[...]
    vn = vn_ref[...]
    head_of_lane = lax.broadcasted_iota(jnp.int32, (N_HEADS, D_ATTN), 1) // HEAD_DIM
    own = head_of_lane == lax.broadcasted_iota(jnp.int32, (N_HEADS, D_ATTN), 0)
    qblk = jnp.where(own, q_ref[...], 0.0)
    s = jnp.dot(qblk.astype(BF16), kt_ref[...].astype(BF16), preferred_element_type=F32)
    s_self = jnp.sum(qblk * kn, axis=1, keepdims=True)

    dist = l_cache - lax.broadcasted_iota(jnp.int32, (N_HEADS, l_cache), 1)
    cnt = jnp.zeros((N_HEADS, l_cache), F32)
    for w, d in zip(WINDOWS, DILATIONS):
        cnt = cnt + jnp.where((dist <= w) & ((dist & (d - 1)) == 0), 1.0, 0.0)
    s = jnp.where(cnt > 0, s, NEG)
    m = jnp.maximum(jnp.max(s, axis=1, keepdims=True), s_self)
    e = (cnt * jnp.exp(s - m)).astype(BF16)
    e_self = len(WINDOWS) * jnp.exp(s_self - m)
    den = jnp.sum(e.astype(F32), axis=1, keepdims=True) + e_self
    pv = lax.dot_general(e, vt_ref[...].astype(BF16), (((1,), (1,)), ((), ())),
                         preferred_element_type=F32)
    per_lane = lambda a: jnp.sum(jnp.where(own, a, 0.0), axis=0, keepdims=True)
    num = per_lane(pv) + per_lane(jnp.broadcast_to(e_self, own.shape)) * vn
    attn_ref[...] = num / per_lane(jnp.broadcast_to(den, own.shape))


def _sample_attn(q, kf, vf, cache_kt, cache_vt, layer):
    _, bd, _, l_cache = cache_kt.shape
    assert l_cache >= WINDOWS[-1]
    row = pl.BlockSpec((None, 1, D_ATTN), lambda b: (b, 0, 0))
    cache = pl.BlockSpec((None, None, D_ATTN, l_cache), lambda b: (layer, b, 0, 0))
    r3 = lambda a: a.reshape(bd, 1, D_ATTN)
    attn = pl.pallas_call(
        _sample_attn_kernel,
        grid=(bd,),
        in_specs=[row, row, row, cache, cache],
        out_specs=row,
        out_shape=jax.ShapeDtypeStruct((bd, 1, D_ATTN), F32),
        compiler_params=pltpu.CompilerParams(dimension_semantics=("arbitrary",),
                                             vmem_limit_bytes=VMEM_LIMIT),
        name="sample_attn",
    )(r3(q), r3(kf), r3(vf), cache_kt, cache_vt)
    return attn.reshape(bd, D_ATTN)


def _sample_out_kernel(x_ref, attn_ref, zs_ref, u_ref, gza_ref, bz_ref, uc_ref, sa_ref, sc_ref,
                       aw_ref, ab_ref, ag_ref, alb_ref, cw_ref, w_ref, y_ref, na_ref, nc_ref):
    u = u_ref[...]
    uc = uc_ref[...]
    na = CONV_WIDTH - 1
    nc = SCONV_WIDTH - 1
    ca = ab_ref[...] + aw_ref[na:na + 1, :] * u
    for j in range(na):
        ca = ca + aw_ref[j:j + 1, :] * sa_ref[:, j * D_CONV:(j + 1) * D_CONV]
    ya = _silu(_layernorm_rows(ca, ag_ref[...], alb_ref[...])) * gza_ref[...]
    cc = cw_ref[nc:nc + 1, :] * uc
    for j in range(nc):
        cc = cc + cw_ref[j:j + 1, :] * sc_ref[:, j * D_SCONV:(j + 1) * D_SCONV]
    yc = bz_ref[...] * cc
    acc = jnp.dot((attn_ref[...] * zs_ref[...]).astype(BF16), w_ref[0:D_ATTN, :],
                  preferred_element_type=F32)
    acc = acc + jnp.dot(ya.astype(BF16), w_ref[D_ATTN:D_ATTN + D_CONV, :],
                        preferred_element_type=F32)
    acc = acc + jnp.dot(yc.astype(BF16), w_ref[D_ATTN + D_CONV:, :],
                        preferred_element_type=F32)
    y_ref[...] = x_ref[...] + acc
    na_ref[:, 0:(na - 1) * D_CONV] = sa_ref[:, D_CONV:na * D_CONV]
    na_ref[:, (na - 1) * D_CONV:] = u
    nc_ref[:, 0:(nc - 1) * D_SCONV] = sc_ref[:, D_SCONV:nc * D_SCONV]
    nc_ref[:, (nc - 1) * D_SCONV:] = uc


def _sample_out(x, attn, zs, u, gza, bz, uc, sa, sc, aw, ab, ag, alb, cw, w_bf, layer):
    depth, bd = sa.shape[:2]
    sa2 = sa.reshape(depth, bd, -1)
    sc2 = sc.reshape(depth, bd, -1)
    per_token = (x, attn, zs, u, gza, bz, uc)
    per_layer = (sa2, sc2, aw, ab, ag, alb, cw, w_bf)
    full = lambda a: pl.BlockSpec(a.shape, lambda i: (0,) * a.ndim)
    y, na, nc = pl.pallas_call(
        _sample_out_kernel,
        grid=(1,),
        in_specs=[full(a) for a in per_token] + [_layer_spec(a, layer) for a in per_layer],
        out_specs=(full(x), pl.BlockSpec(sa2.shape[1:], lambda i: (0, 0)),
                   pl.BlockSpec(sc2.shape[1:], lambda i: (0, 0))),
        out_shape=(jax.ShapeDtypeStruct(x.shape, F32),
                   jax.ShapeDtypeStruct(sa2.shape[1:], F32),
                   jax.ShapeDtypeStruct(sc2.shape[1:], F32)),
        compiler_params=pltpu.CompilerParams(dimension_semantics=("arbitrary",),
                                             vmem_limit_bytes=VMEM_LIMIT),
        name="sample_out",
    )(*per_token, *per_layer)
    return y, na.reshape(sa.shape[1:]), nc.reshape(sc.shape[1:])


def _rope_tables(pos):
    half = HEAD_DIM // 2
    inv = ROPE_THETA ** (-jnp.arange(half, dtype=F32) / half)
    ang = pos.astype(F32)[:, None] * inv[None, :]
    cos, sin = jnp.cos(ang), jnp.sin(ang)
    reps = LANES // HEAD_DIM
    cos_t = jnp.tile(jnp.concatenate([cos, cos], axis=-1), (1, reps))
    sin_t = jnp.tile(jnp.concatenate([-sin, sin], axis=-1), (1, reps))
    return cos_t, sin_t


def _segment_mean_matrix():
    seg = np.kron(np.eye(LANES // HEAD_DIM), np.ones((HEAD_DIM, HEAD_DIM))) / HEAD_DIM
    return jnp.asarray(seg, dtype=BF16)


def _to_channel_major(cache):
    lead = cache.shape[:-3]
    n = len(lead)
    perm = tuple(range(n)) + (n + 1, n + 2, n)
    return jnp.transpose(cache, perm).reshape(*lead, D_ATTN, cache.shape[-3])


def _from_channel_major(cache_t):
    lead = cache_t.shape[:-2]
    n = len(lead)
    perm = tuple(range(n)) + (n + 2, n, n + 1)
    return jnp.transpose(cache_t.reshape(*lead, N_HEADS, HEAD_DIM, cache_t.shape[-1]), perm)


def kernel(x_prompt, x_sample, cache_k, cache_v, state_conv_a, state_conv_c, ln_g, w_in,
           q_norm_g, k_norm_g, a_conv_w, a_conv_b, a_ln_g, a_ln_b, c_conv_w, w_out):
    bp, sp, _ = x_prompt.shape
    bd, ts, _ = x_sample.shape
    assert ts == 1
    depth = w_in.shape[0]
    l_prompt = min(WINDOWS[-1], sp)

    cos_p, sin_p = _rope_tables(jnp.arange(sp, dtype=jnp.int32))
    cos_s, sin_s = _rope_tables(jnp.full((bd,), PAST_LEN, dtype=jnp.int32))
    seg = _segment_mean_matrix()
    cache_kt = _to_channel_major(cache_k)
    cache_vt = _to_channel_major(cache_v)

    rows = lambda a: a.reshape(depth, 1, -1)
    w_in_bf = w_in.astype(BF16)
    w_out_bf = w_out.astype(BF16)
    lng = rows(ln_g)
    qg = rows(jnp.tile(q_norm_g, (1, LANES // HEAD_DIM)))
    kg = rows(jnp.tile(k_norm_g, (1, LANES // HEAD_DIM)))
    conv_w = (a_conv_w, rows(a_conv_b), rows(a_ln_g), rows(a_ln_b), c_conv_w)
    aw8 = jnp.broadcast_to(a_conv_w[:, :, None, :], (depth, CONV_WIDTH, SUBLANES, D_CONV))

    hs = x_sample.reshape(bd, D_MODEL)
    sk, sv, sa, sc = [], [], [], []
    for l in range(depth):
        q, kf, vf, zs, u, gza, bz, uc = _in_proj_sample(hs, lng, w_in_bf, cos_s, sin_s, qg, kg,
                                                        seg, l)
        attn = _sample_attn(q, kf, vf, cache_kt, cache_vt, l)
        hs, na, nc = _sample_out(hs, attn, zs, u, gza, bz, uc, state_conv_a, state_conv_c,
                                 *conv_w, w_out_bf, l)
        sk.append(kf)
        sv.append(vf)
        sa.append(na)
        sc.append(nc)
    new_cols = lambda new_rows: jnp.transpose(jnp.stack(new_rows), (0, 2, 1))

    assert depth >= 2
    shift_jobs = {depth - 2: (cache_kt, new_cols(sk)), depth - 1: (cache_vt, new_cols(sv))}
    shifted = {}

    hp = x_prompt
    pk, pv, pa, pc = [], [], [], []
    for l in range(depth):
        outs = _in_proj_prompt(hp, lng, w_in_bf, cos_p, sin_p, qg, kg, seg, l, PROMPT_TM_IN,
                               l_prompt, shift=shift_jobs.get(l))
        if l in shift_jobs:
            outs, shifted[l] = outs[:-1], outs[-1]
        n_pat = len(DILATIONS)
        q_d, k_d, v_d = outs[0:n_pat], outs[n_pat:2 * n_pat], outs[2 * n_pat:3 * n_pat]
        kt, vt, zs, u, gza, bz, uc = outs[3 * n_pat:]
        attn = _attention(list(zip(q_d, k_d, v_d)), bp, sp)
        hp = _out_proj(hp, attn, zs, u, gza, bz, uc, aw8, *conv_w[1:], w_out_bf, l,
                       PROMPT_TM_OUT)
        pk.append(kt)
        pv.append(vt)
        pa.append(u[:, sp - (CONV_WIDTH - 1):])
        pc.append(uc[:, sp - (SCONV_WIDTH - 1):])

    return (hp, hs.reshape(bd, ts, D_MODEL),
            _from_channel_major(jnp.stack(pk)), _from_channel_major(jnp.stack(pv)),
            jnp.stack(pa), jnp.stack(pc),
            _from_channel_major(shifted[depth - 2]), _from_channel_major(shifted[depth - 1]),
            jnp.stack(sa), jnp.stack(sc))
```

```python
import functools

import numpy as np
import jax
import jax.numpy as jnp
from jax import lax
from jax.experimental import pallas as pl
from jax.experimental.pallas import tpu as pltpu

D_MODEL = 1024
N_HEADS = 8
HEAD_DIM = 64
D_ATTN = N_HEADS * HEAD_DIM
D_CONV = 256
CONV_WIDTH = 31
D_SCONV = 256
SCONV_WIDTH = 3
WINDOWS = (128, 512, 2048)
DILATIONS = (1, 4, 16)
WIN_KEYS = 128
assert all(w // d == WIN_KEYS for w, d in zip(WINDOWS, DILATIONS))
assert all(d & (d - 1) == 0 for d in DILATIONS)
assert len(DILATIONS) == 3 and DILATIONS[0] == 1 and DILATIONS[2] % DILATIONS[1] == 0
PAST_LEN = 16384
ROPE_THETA = 10000.0
EPS = 1e-6
NEG = -1e30
LOG2_E = 1.4426950408889634

C_Q, C_K, C_V, C_Z = 0, 512, 1024, 1536
C_AVAL, C_AGATE, C_ZA = 2048, 2304, 2560
C_CB, C_CC, C_CH, C_ZC = 2816, 3072, 3328, 3584
D_IN = 3840

LANES = 128
SUBLANES = 8
N_SLABS = D_ATTN // LANES
VMEM_LIMIT = 56 * 1024 * 1024

PROMPT_TM_IN = 512
PROMPT_TM_OUT = 1024
KT_DILATIONS = tuple(d for d in DILATIONS if (PROMPT_TM_IN // d) % LANES == 0)

F32 = jnp.float32
BF16 = jnp.bfloat16


def _layer_spec(a, layer, **kwargs):
    return pl.BlockSpec((None,) + a.shape[1:], lambda *_: (layer,) + (0,) * (a.ndim - 1), **kwargs)


def _sigmoid(x):
    return 1.0 / (1.0 + jnp.exp(-x))


def _silu(x):
    return x * _sigmoid(x)


SHIFT_ROWS = 32


def _shift_cache_block(c_ref, nt_ref, n_ref, b):
    n_rows, l_cache = c_ref.shape
    bd = nt_ref.shape[1]
    mine = lax.broadcasted_iota(jnp.int32, (SHIFT_ROWS, bd), 1) == b
    last_lane = lax.broadcasted_iota(jnp.int32, (SHIFT_ROWS, LANES), 1) == LANES - 1
    for c in range(n_rows // SHIFT_ROWS):
        rows = slice(c * SHIFT_ROWS, (c + 1) * SHIFT_ROWS)
        new_col = jnp.sum(jnp.where(mine, nt_ref[rows, :], 0.0), axis=1, keepdims=True)
        rolled = pltpu.roll(c_ref[rows, :], l_cache - 1, 1)
        n_ref[rows, 0:l_cache - LANES] = rolled[:, 0:l_cache - LANES]
        n_ref[rows, l_cache - LANES:] = jnp.where(last_lane, new_col, rolled[:, l_cache - LANES:])


def _shift_specs(depth, bd, l_cache, step_of):
    cache = pl.BlockSpec((None, None, D_ATTN, l_cache),
                         lambda *g: (step_of(*g) // bd, step_of(*g) % bd, 0, 0))
    cols = pl.BlockSpec((None, D_ATTN, bd), lambda *g: (step_of(*g) // bd, 0, 0))
    return cache, cols


def _in_proj_body(x_ref, lng_ref, w_ref, cos_ref, sin_ref, qg_ref, kg_ref, seg_ref,
                  zs_ref, u_ref, gza_ref, bz_ref, uc_ref, emit_q, emit_k, emit_v, q_scale):
    x = x_ref[...]
    ms = jnp.mean(x * x, axis=-1, keepdims=True)
    xn = (x * lax.rsqrt(ms + EPS) * lng_ref[...]).astype(BF16)

    def proj(c0, c1):
        return jnp.dot(xn, w_ref[:, c0:c1], preferred_element_type=F32)

    cos = cos_ref[...]
    sin = sin_ref[...]
    tm = x.shape[0]
    first_half = (lax.broadcasted_iota(jnp.int32, (tm, LANES), 1) & (HEAD_DIM // 2)) == 0

    def norm_rope(p, g_ref, scale, emit):
        for s in range(N_SLABS):
            ps = p[:, s * LANES:(s + 1) * LANES]
            ss = jnp.dot((ps * ps).astype(BF16), seg_ref[...], preferred_element_type=F32)
            pn = ps * lax.rsqrt(ss + EPS) * g_ref[...]
            partner = jnp.where(first_half,
                                pltpu.roll(pn, LANES - HEAD_DIM // 2, 1),
                                pltpu.roll(pn, HEAD_DIM // 2, 1))
            emit(s, (pn * cos + partner * sin) * scale)

    norm_rope(proj(C_Q, C_K), qg_ref, q_scale, emit_q)
    norm_rope(proj(C_K, C_V), kg_ref, 1.0, emit_k)
    v = proj(C_V, C_Z)
    for s in range(N_SLABS):
        emit_v(s, v[:, s * LANES:(s + 1) * LANES])
    zs_ref[...] = _silu(proj(C_Z, C_AVAL)).astype(zs_ref.dtype)
    ag = proj(C_AVAL, C_ZA)
    u_ref[...] = ag[:, :D_CONV] * _sigmoid(ag[:, D_CONV:])
    zb = proj(C_ZA, C_CC)
    gza_ref[...] = _silu(zb[:, :D_CONV]).astype(gza_ref.dtype)
    zc = proj(C_ZC, D_IN)
    bz_ref[...] = (zb[:, D_CONV:] * _silu(zc)).astype(bz_ref.dtype)
    ch = proj(C_CC, C_ZC)
    uc_ref[...] = ch[:, :D_SCONV] * ch[:, D_SCONV:]


def _in_proj_prompt_kernel(x_ref, lng_ref, w_ref, cos_ref, sin_ref, qg_ref, kg_ref, seg_ref,
                           *rest, tm, shift):
    if shift:
        (c_ref, nt_ref, q1_ref, q4_ref, q16_ref, k1_ref, k4_ref, k16_ref, v1_ref, v4_ref, v16_ref,
         kt_ref, vt_ref, zs_ref, u_ref, gza_ref, bz_ref, uc_ref, n_ref,
         stage_ref, stage_next_ref) = rest
        grid_step = pl.program_id(0) * pl.num_programs(1) + pl.program_id(1)
        _shift_cache_block(c_ref, nt_ref, n_ref, grid_step % nt_ref.shape[1])
    else:
        (q1_ref, q4_ref, q16_ref, k1_ref, k4_ref, k16_ref, v1_ref, v4_ref, v16_ref,
         kt_ref, vt_ref, zs_ref, u_ref, gza_ref, bz_ref, uc_ref,
         stage_ref, stage_next_ref) = rest

    def emitter(which, outs, t_ref, transposed=()):
        def emit(s, val):
            slot = (which * N_SLABS + s) % 2
            levels = {1: stage_ref.at[slot]}
            levels[1][0] = val
            val_t = val.T if t_ref is not None else None
            for li, (d, out) in enumerate(zip(DILATIONS, outs)):
                prev = DILATIONS[li - 1] if li else 1
                keep = li + 1 < len(DILATIONS)
                for r in range(d):
                    if d == 1:
                        part = val
                    else:
                        part = levels[prev][r % prev, pl.ds(r // prev, tm // d, stride=d // prev), :]
                        if keep:
                            stage_next_ref[slot, r] = part
                    if d in transposed:
                        part = val_t if d == 1 else part.T
                    out[s, r] = part.astype(BF16)
                if d != 1 and keep:
                    levels[d] = stage_next_ref.at[slot]
            if t_ref is not None:
                t_ref[s * LANES:(s + 1) * LANES, :] = val_t
        return emit

    _in_proj_body(x_ref, lng_ref, w_ref, cos_ref, sin_ref, qg_ref, kg_ref, seg_ref,
                  zs_ref, u_ref, gza_ref, bz_ref, uc_ref,
                  emitter(0, (q1_ref, q4_ref, q16_ref), None),
                  emitter(1, (k1_ref, k4_ref, k16_ref), kt_ref, transposed=KT_DILATIONS),
                  emitter(2, (v1_ref, v4_ref, v16_ref), vt_ref),
                  q_scale=HEAD_DIM ** -0.5 * LOG2_E)


def _in_proj_prompt(x, lng, w_bf, cos_t, sin_t, qg, kg, seg, layer, tm, l_tail, shift=None):
    b, s, _ = x.shape
    grid = (b, s // tm)
    n_skip = (s - l_tail) // tm
    tile = lambda c: pl.BlockSpec((None, tm, c), lambda bi, i: (bi, i, 0))
    full = lambda a: pl.BlockSpec(a.shape, lambda bi, i: (0,) * a.ndim)
    pos = pl.BlockSpec((tm, LANES), lambda bi, i: (i, 0))
    def deint(d, transposed):
        blk = (LANES, tm // d) if transposed else (tm // d, LANES)
        idx = (lambda bi, i: (bi, 0, 0, 0, i)) if transposed else (lambda bi, i: (bi, 0, 0, i, 0))
        dims = (LANES, s // d) if transposed else (s // d, LANES)
        return (pl.BlockSpec((None, N_SLABS, d) + blk, idx),
                jax.ShapeDtypeStruct((b, N_SLABS, d) + dims, BF16))

    qkv = [deint(d, which == 1 and d in KT_DILATIONS) for which in range(3) for d in DILATIONS]
    tail = pl.BlockSpec((None, D_ATTN, tm), lambda bi, i: (bi, 0, jnp.maximum(i - n_skip, 0)))
    out_shapes = tuple(shape for _, shape in qkv) + (
        jax.ShapeDtypeStruct((b, D_ATTN, l_tail), F32),
        jax.ShapeDtypeStruct((b, D_ATTN, l_tail), F32),
        jax.ShapeDtypeStruct((b, s, D_ATTN), BF16),
        jax.ShapeDtypeStruct((b, s, D_CONV), F32),
        jax.ShapeDtypeStruct((b, s, D_CONV), BF16),
        jax.ShapeDtypeStruct((b, s, D_SCONV), BF16),
        jax.ShapeDtypeStruct((b, s, D_SCONV), F32),
    )
    out_specs = tuple(spec for spec, _ in qkv) + (
        tail, tail, tile(D_ATTN), tile(D_CONV), tile(D_CONV), tile(D_SCONV), tile(D_SCONV))
    w_spec = _layer_spec(w_bf, layer, pipeline_mode=pl.Buffered(1))
    in_specs = [tile(D_MODEL), _layer_spec(lng, layer), w_spec, pos, pos,
                _layer_spec(qg, layer), _layer_spec(kg, layer), full(seg)]
    args = [x, lng, w_bf, cos_t, sin_t, qg, kg, seg]
    if shift is not None:
        cache, cols = shift
        depth, bd, _, l_cache = cache.shape
        assert depth * bd == grid[0] * grid[1]
        cache_spec, cols_spec = _shift_specs(depth, bd, l_cache, lambda bi, i: bi * grid[1] + i)
        in_specs += [cache_spec, cols_spec]
        args += [cache, cols]
        out_specs += (cache_spec,)
        out_shapes += (jax.ShapeDtypeStruct(cache.shape, F32),)
    return pl.pallas_call(
        functools.partial(_in_proj_prompt_kernel, tm=tm, shift=shift is not None),
        grid=grid,
        in_specs=in_specs,
        out_specs=out_specs,
        out_shape=out_shapes,
        scratch_shapes=[pltpu.VMEM((2, 1, tm, LANES), F32),
                        pltpu.VMEM((2, DILATIONS[1], tm // DILATIONS[1], LANES), F32)],
        compiler_params=pltpu.CompilerParams(dimension_semantics=("arbitrary", "arbitrary"),
                                             vmem_limit_bytes=VMEM_LIMIT),
        name="in_proj",
    )(*args)


def _in_proj_sample_kernel(x_ref, lng_ref, w_ref, cos_ref, sin_ref, qg_ref, kg_ref, seg_ref,
                           q_ref, k_ref, v_ref, zs_ref, u_ref, gza_ref, bz_ref, uc_ref):
    def emitter(out):
        def emit(s, val):
            out[:, s * LANES:(s + 1) * LANES] = val
        return emit

    _in_proj_body(x_ref, lng_ref, w_ref, cos_ref, sin_ref, qg_ref, kg_ref, seg_ref,
                  zs_ref, u_ref, gza_ref, bz_ref, uc_ref,
                  emitter(q_ref), emitter(k_ref), emitter(v_ref), q_scale=HEAD_DIM ** -0.5)


def _in_proj_sample(x2d, lng, w_bf, cos_t, sin_t, qg, kg, seg, layer):
    n = x2d.shape[0]
    args = (x2d, lng, w_bf, cos_t, sin_t, qg, kg, seg)
    full = lambda a: pl.BlockSpec(a.shape, lambda i: (0,) * a.ndim)
    per_layer = lambda a: _layer_spec(a, layer)
    widths = (D_ATTN, D_ATTN, D_ATTN, D_ATTN, D_CONV, D_CONV, D_SCONV, D_SCONV)
    out_shapes = tuple(jax.ShapeDtypeStruct((n, c), F32) for c in widths)
    return pl.pallas_call(
        _in_proj_sample_kernel,
        grid=(1,),
        in_specs=[full(x2d), per_layer(lng), per_layer(w_bf), full(cos_t), full(sin_t),
                  per_layer(qg), per_layer(kg), full(seg)],
        out_specs=tuple(full(s) for s in out_shapes),
        out_shape=out_shapes,
        compiler_params=pltpu.CompilerParams(dimension_semantics=("arbitrary",),
                                             vmem_limit_bytes=VMEM_LIMIT),
        name="in_proj_sample",
    )(*args)


ATTN_TILE = 2048
assert all(ATTN_TILE % (WIN_KEYS * d) == 0 for d in DILATIONS)
ATTN_UNROLL = 8
assert (ATTN_TILE // WIN_KEYS) % ATTN_UNROLL == 0


def _attn_kernel(*refs, tile):
    n_pat = len(DILATIONS)
    qkv_refs = [refs[3 * p:3 * p + 3] for p in range(n_pat)]
    bias_ref, o_ref, acc_ref, m_ref, l_ref = refs[3 * n_pat:]
    t = pl.program_id(2)
    blk = WIN_KEYS
    n_blocks = tile // blk
    lane = lax.broadcasted_iota(jnp.int32, (1, LANES), 1)
    head_a = lane < HEAD_DIM

    def block_softmax(qb, k_ref, v_ref, r, q0, k_channel_major):
        kk = pl.multiple_of(jnp.maximum(q0 - blk, 0), blk)
        bias = bias_ref[jnp.minimum(q0, 1)]
        if k_channel_major:
            kb = k_ref[r, :, pl.ds(kk, 2 * blk)]
            contract = (((1,), (0,)), ((), ()))
        else:
            kb = k_ref[r, pl.ds(kk, 2 * blk), :]
            contract = (((1,), (1,)), ((), ()))
        vb = v_ref[r, pl.ds(kk, 2 * blk), :]
        res, mx = [], []
        for sel in (head_a, jnp.logical_not(head_a)):
            qh = jnp.where(sel, qb, jnp.zeros_like(qb))
            s = lax.dot_general(qh, kb, contract, preferred_element_type=F32)
            s = s + bias
            m = jnp.max(s, axis=1, keepdims=True)
            p = jnp.exp2(s - m).astype(BF16)
            vh = jnp.where(sel, vb, jnp.ones_like(vb))
            res.append(jnp.dot(p, vh, preferred_element_type=F32))
            mx.append(m)
        num = jnp.where(head_a, res[0], res[1])
        den = pltpu.roll(jnp.where(head_a, res[1], res[0]), HEAD_DIM, 1)
        return num, den, jnp.where(head_a, mx[0], mx[1])

    order = sorted(range(n_pat), key=lambda p: -DILATIONS[p])
    for step, p in enumerate(order):
        d = DILATIONS[p]
        q_ref, k_ref, v_ref = qkv_refs[p]
        per_res = n_blocks // d
        assert per_res & (per_res - 1) == 0
        first, last = step == 0, step == n_pat - 1

        def body(it, carry, d=d, q_ref=q_ref, k_ref=k_ref, v_ref=v_ref, per_res=per_res,
                 first=first, last=last):
            for u in range(ATTN_UNROLL):
                i = it * ATTN_UNROLL + u
                r = lax.shift_right_logical(i, per_res.bit_length() - 1)
                jb = i & (per_res - 1)
                qb = q_ref[r, pl.ds(pl.multiple_of(jb * blk, blk), blk), :]
                num, den, mb = block_softmax(qb, k_ref, v_ref, r, t * (tile // d) + jb * blk,
                                             d in KT_DILATIONS)
                start = jb * blk * d + r
                rows = pl.ds(start, blk) if d == 1 else pl.ds(start, blk, stride=d)
                if not first:
                    m_old = m_ref[rows, :]
                    m_new = jnp.maximum(m_old, mb)
                    w_old = jnp.exp2(m_old - m_new)
                    w_blk = jnp.exp2(mb - m_new)
                    num = acc_ref[rows, :] * w_old + num * w_blk
                    den = l_ref[rows, :] * w_old + den * w_blk
                    mb = m_new
                if last:
                    o_ref[rows, :] = num / den
                else:
                    acc_ref[rows, :] = num
                    l_ref[rows, :] = den
                    m_ref[rows, :] = mb
            return carry

        lax.fori_loop(0, n_blocks // ATTN_UNROLL, body, 0)


def _band_bias():
    rel = np.arange(2 * WIN_KEYS)[None, :] - np.arange(WIN_KEYS)[:, None]
    variants = [rel - off for off in (0, WIN_KEYS)]
    return jnp.asarray(np.stack([np.where((v <= 0) & (v >= -WIN_KEYS), 0.0, NEG)
                                 for v in variants]), dtype=F32)


def _attention(qkv, b, s):
    tile = ATTN_TILE
    in_specs, args = [], []
    for d, (q, k, v) in zip(DILATIONS, qkv):
        qspec = pl.BlockSpec((None, None, d, tile // d, LANES), lambda bi, c, i: (bi, c, 0, i, 0))
        kvspec = pl.BlockSpec((None, None, d, s // d, LANES), lambda bi, c, i: (bi, c, 0, 0, 0))
        ktspec = pl.BlockSpec((None, None, d, LANES, s // d), lambda bi, c, i: (bi, c, 0, 0, 0))
        in_specs += [qspec, ktspec if d in KT_DILATIONS else kvspec, kvspec]
        args += [q, k, v]
    bias = _band_bias()
    in_specs.append(pl.BlockSpec(bias.shape, lambda bi, c, i: (0, 0, 0)))
    args.append(bias)
    return pl.pallas_call(
        functools.partial(_attn_kernel, tile=tile),
        grid=(b, N_SLABS, s // tile),
        in_specs=in_specs,
        out_specs=pl.BlockSpec((None, tile, LANES), lambda bi, c, i: (bi, i, c)),
        out_shape=jax.ShapeDtypeStruct((b, s, D_ATTN), F32),
        scratch_shapes=[pltpu.VMEM((tile, LANES), F32)] * 3,
        compiler_params=pltpu.CompilerParams(
            dimension_semantics=("arbitrary", "arbitrary", "arbitrary"),
            vmem_limit_bytes=VMEM_LIMIT),
        name="dilated_attn",
    )(*args)


A_HALO = 32
C_HALO = 8
OUT_CHUNK = 256


def _layernorm_rows(x, g, b):
    mu = jnp.mean(x, axis=-1, keepdims=True)
    xc = x - mu
    var = jnp.mean(xc * xc, axis=-1, keepdims=True)
    return xc * lax.rsqrt(var + EPS) * g + b


def _out_proj_kernel(x_ref, attn_ref, zs_ref, u_ref, uh_ref, gza_ref, bz_ref, uc_ref, uch_ref,
                     aw_ref, ab_ref, ag_ref, alb_ref, cw_ref, w_ref,
                     y_ref, ubuf, uphase, ucbuf, *, tm):
    first_tile = pl.program_id(1) == 0
    y_ref[...] = x_ref[...] + jnp.dot((attn_ref[...] * zs_ref[...]).astype(BF16),
                                      w_ref[0:D_ATTN, :], preferred_element_type=F32)

    ubuf[0:A_HALO, :] = jnp.where(first_tile, 0.0, uh_ref[...])
    ubuf[A_HALO:A_HALO + tm, :] = u_ref[...]
    n_rows = tm + A_HALO - SUBLANES
    for b in range(1, SUBLANES):
        uphase[b - 1] = ubuf[b:b + n_rows, :]
    ucbuf[0:C_HALO, :] = jnp.where(first_tile, 0.0, uch_ref[...])
    ucbuf[C_HALO:C_HALO + tm, :] = uc_ref[...]

    base = A_HALO - (CONV_WIDTH - 1)
    cbase = C_HALO - (SCONV_WIDTH - 1)
    groups = (OUT_CHUNK // SUBLANES, SUBLANES, D_CONV)
    for c in range(tm // OUT_CHUNK):
        r0 = c * OUT_CHUNK
        rows = slice(r0, r0 + OUT_CHUNK)
        ca = jnp.zeros(groups, F32)
        for j in range(CONV_WIDTH):
            b = (base + j) % SUBLANES
            a8 = base + j - b + r0
            src = (ubuf[a8:a8 + OUT_CHUNK, :] if b == 0
                   else uphase[b - 1, a8:a8 + OUT_CHUNK, :])
            ca = ca + aw_ref[j][None] * src.reshape(groups)
        ca = ca.reshape(OUT_CHUNK, D_CONV) + ab_ref[...]
        ya = _silu(_layernorm_rows(ca, ag_ref[...], alb_ref[...])) * gza_ref[rows, :]
        cc = jnp.zeros((OUT_CHUNK, D_SCONV), F32)
        for j in range(SCONV_WIDTH):
            cc = cc + cw_ref[j:j + 1, :] * ucbuf[cbase + j + r0:cbase + j + r0 + OUT_CHUNK, :]
        yc = bz_ref[rows, :] * cc
        y_ref[rows, :] += (
            jnp.dot(ya.astype(BF16), w_ref[D_ATTN:D_ATTN + D_CONV, :], preferred_element_type=F32)
            + jnp.dot(yc.astype(BF16), w_ref[D_ATTN + D_CONV:, :], preferred_element_type=F32))


def _out_proj(x, attn, zs, u, gza, bz, uc, aw, ab, ag, alb, cw, w_bf, layer, tm):
    b, s, _ = x.shape
    tile = lambda c: pl.BlockSpec((None, tm, c), lambda bi, i: (bi, i, 0))
    halo = lambda h, c: pl.BlockSpec(
        (None, h, c), lambda bi, i: (bi, jnp.maximum(i * (tm // h) - 1, 0), 0))
    return pl.pallas_call(
        functools.partial(_out_proj_kernel, tm=tm),
        grid=(b, s // tm),
        in_specs=[tile(D_MODEL), tile(D_ATTN), tile(D_ATTN),
                  tile(D_CONV), halo(A_HALO, D_CONV), tile(D_CONV), tile(D_SCONV),
                  tile(D_SCONV), halo(C_HALO, D_SCONV)]
                 + [_layer_spec(a, layer) for a in (aw, ab, ag, alb, cw, w_bf)],
        out_specs=tile(D_MODEL),
        out_shape=jax.ShapeDtypeStruct((b, s, D_MODEL), F32),
        scratch_shapes=[pltpu.VMEM((A_HALO + tm, D_CONV), F32),
                        pltpu.VMEM((SUBLANES - 1, A_HALO + tm - SUBLANES, D_CONV), F32),
                        pltpu.VMEM((C_HALO + tm, D_SCONV), F32)],
        compiler_params=pltpu.CompilerParams(dimension_semantics=("arbitrary", "arbitrary"),
                                             vmem_limit_bytes=VMEM_LIMIT),
        name="out_proj",
    )(x, attn, zs, u, u, gza, bz, uc, uc, aw, ab, ag, alb, cw, w_bf)


def _sample_attn_kernel(q_ref, kn_ref, vn_ref, kt_ref, vt_ref, attn_ref):
    l_cache = kt_ref.shape[1]
    kn = kn_ref[...]
    vn = vn_ref[...]
    head_of_lane = lax.broadcasted_iota(jnp.int32, (N_HEADS, D_ATTN), 1) // HEAD_DIM
    own = head_of_lane == lax.broadcasted_iota(jnp.int32, (N_HEADS, D_ATTN), 0)
    qblk = jnp.where(own, q_ref[...], 0.0)
    s = jnp.dot(qblk.astype(BF16), kt_ref[...].astype(BF16), preferred_element_type=F32)
    s_self = jnp.sum(qblk * kn, axis=1, keepdims=True)

    dist = l_cache - lax.broadcasted_iota(jnp.int32, (N_HEADS, l_cache), 1)
    cnt = jnp.zeros((N_HEADS, l_cache), F32)
    for w, d in zip(WINDOWS, DILATIONS):
        cnt = cnt + jnp.where((dist <= w) & ((dist & (d - 1)) == 0), 1.0, 0.0)
    s = jnp.where(cnt > 0, s, NEG)
    m = jnp.maximum(jnp.max(s, axis=1, keepdims=True), s_self)
    e = (cnt * jnp.exp(s - m)).astype(BF16)
    e_self = len(WINDOWS) * jnp.exp(s_self - m)
    den = jnp.sum(e.astype(F32), axis=1, keepdims=True) + e_self
    pv = lax.dot_general(e, vt_ref[...].astype(BF16), (((1,), (1,)), ((), ())),
                         preferred_element_type=F32)
    per_lane = lambda a: jnp.sum(jnp.where(own, a, 0.0), axis=0, keepdims=True)
    num = per_lane(pv) + per_lane(jnp.broadcast_to(e_self, own.shape)) * vn
    attn_ref[...] = num / per_lane(jnp.broadcast_to(den, own.shape))


def _sample_attn(q, kf, vf, cache_kt, cache_vt, layer):
    _, bd, _, l_cache = cache_kt.shape
    assert l_cache >= WINDOWS[-1]
    row = pl.BlockSpec((None, 1, D_ATTN), lambda b: (b, 0, 0))
    cache = pl.BlockSpec((None, None, D_ATTN, l_cache), lambda b: (layer, b, 0, 0))
    r3 = lambda a: a.reshape(bd, 1, D_ATTN)
    attn = pl.pallas_call(
        _sample_attn_kernel,
        grid=(bd,),
        in_specs=[row, row, row, cache, cache],
        out_specs=row,
        out_shape=jax.ShapeDtypeStruct((bd, 1, D_ATTN), F32),
        compiler_params=pltpu.CompilerParams(dimension_semantics=("arbitrary",),
                                             vmem_limit_bytes=VMEM_LIMIT),
        name="sample_attn",
    )(r3(q), r3(kf), r3(vf), cache_kt, cache_vt)
    return attn.reshape(bd, D_ATTN)


def _sample_out_kernel(x_ref, attn_ref, zs_ref, u_ref, gza_ref, bz_ref, uc_ref, sa_ref, sc_ref,
                       aw_ref, ab_ref, ag_ref, alb_ref, cw_ref, w_ref, y_ref, na_ref, nc_ref):
    u = u_ref[...]
    uc = uc_ref[...]
    na = CONV_WIDTH - 1
    nc = SCONV_WIDTH - 1
    ca = ab_ref[...] + aw_ref[na:na + 1, :] * u
    for j in range(na):
        ca = ca + aw_ref[j:j + 1, :] * sa_ref[:, j * D_CONV:(j + 1) * D_CONV]
    ya = _silu(_layernorm_rows(ca, ag_ref[...], alb_ref[...])) * gza_ref[...]
    cc = cw_ref[nc:nc + 1, :] * uc
    for j in range(nc):
        cc = cc + cw_ref[j:j + 1, :] * sc_ref[:, j * D_SCONV:(j + 1) * D_SCONV]
    yc = bz_ref[...] * cc
    acc = jnp.dot((attn_ref[...] * zs_ref[...]).astype(BF16), w_ref[0:D_ATTN, :],
                  preferred_element_type=F32)
    acc = acc + jnp.dot(ya.astype(BF16), w_ref[D_ATTN:D_ATTN + D_CONV, :],
                        preferred_element_type=F32)
    acc = acc + jnp.dot(yc.astype(BF16), w_ref[D_ATTN + D_CONV:, :],
                        preferred_element_type=F32)
    y_ref[...] = x_ref[...] + acc
    na_ref[:, 0:(na - 1) * D_CONV] = sa_ref[:, D_CONV:na * D_CONV]
    na_ref[:, (na - 1) * D_CONV:] = u
    nc_ref[:, 0:(nc - 1) * D_SCONV] = sc_ref[:, D_SCONV:nc * D_SCONV]
    nc_ref[:, (nc - 1) * D_SCONV:] = uc


def _sample_out(x, attn, zs, u, gza, bz, uc, sa, sc, aw, ab, ag, alb, cw, w_bf, layer):
    depth, bd = sa.shape[:2]
    sa2 = sa.reshape(depth, bd, -1)
    sc2 = sc.reshape(depth, bd, -1)
    per_token = (x, attn, zs, u, gza, bz, uc)
    per_layer = (sa2, sc2, aw, ab, ag, alb, cw, w_bf)
    full = lambda a: pl.BlockSpec(a.shape, lambda i: (0,) * a.ndim)
    y, na, nc = pl.pallas_call(
        _sample_out_kernel,
        grid=(1,),
        in_specs=[full(a) for a in per_token] + [_layer_spec(a, layer) for a in per_layer],
        out_specs=(full(x), pl.BlockSpec(sa2.shape[1:], lambda i: (0, 0)),
                   pl.BlockSpec(sc2.shape[1:], lambda i: (0, 0))),
        out_shape=(jax.ShapeDtypeStruct(x.shape, F32),
                   jax.ShapeDtypeStruct(sa2.shape[1:], F32),
                   jax.ShapeDtypeStruct(sc2.shape[1:], F32)),
        compiler_params=pltpu.CompilerParams(dimension_semantics=("arbitrary",),
                                             vmem_limit_bytes=VMEM_LIMIT),
        name="sample_out",
    )(*per_token, *per_layer)
    return y, na.reshape(sa.shape[1:]), nc.reshape(sc.shape[1:])


def _rope_tables(pos):
    half = HEAD_DIM // 2
    inv = ROPE_THETA ** (-jnp.arange(half, dtype=F32) / half)
    ang = pos.astype(F32)[:, None] * inv[None, :]
    cos, sin = jnp.cos(ang), jnp.sin(ang)
    reps = LANES // HEAD_DIM
    cos_t = jnp.tile(jnp.concatenate([cos, cos], axis=-1), (1, reps))
    sin_t = jnp.tile(jnp.concatenate([-sin, sin], axis=-1), (1, reps))
    return cos_t, sin_t


def _segment_mean_matrix():
    seg = np.kron(np.eye(LANES // HEAD_DIM), np.ones((HEAD_DIM, HEAD_DIM))) / HEAD_DIM
    return jnp.asarray(seg, dtype=BF16)


def _to_channel_major(cache):
    lead = cache.shape[:-3]
    n = len(lead)
    perm = tuple(range(n)) + (n + 1, n + 2, n)
    return jnp.transpose(cache, perm).reshape(*lead, D_ATTN, cache.shape[-3])


def _from_channel_major(cache_t):
    lead = cache_t.shape[:-2]
    n = len(lead)
    perm = tuple(range(n)) + (n + 2, n, n + 1)
    return jnp.transpose(cache_t.reshape(*lead, N_HEADS, HEAD_DIM, cache_t.shape[-1]), perm)


def kernel(x_prompt, x_sample, cache_k, cache_v, state_conv_a, state_conv_c, ln_g, w_in,
           q_norm_g, k_norm_g, a_conv_w, a_conv_b, a_ln_g, a_ln_b, c_conv_w, w_out):
    bp, sp, _ = x_prompt.shape
    bd, ts, _ = x_sample.shape
    assert ts == 1
    depth = w_in.shape[0]
    l_prompt = min(WINDOWS[-1], sp)

    cos_p, sin_p = _rope_tables(jnp.arange(sp, dtype=jnp.int32))
    cos_s, sin_s = _rope_tables(jnp.full((bd,), PAST_LEN, dtype=jnp.int32))
    seg = _segment_mean_matrix()
    cache_kt = _to_channel_major(cache_k)
    cache_vt = _to_channel_major(cache_v)

    rows = lambda a: a.reshape(depth, 1, -1)
    w_in_bf = w_in.astype(BF16)
    w_out_bf = w_out.astype(BF16)
    lng = rows(ln_g)
    qg = rows(jnp.tile(q_norm_g, (1, LANES // HEAD_DIM)))
    kg = rows(jnp.tile(k_norm_g, (1, LANES // HEAD_DIM)))
    conv_w = (a_conv_w, rows(a_conv_b), rows(a_ln_g), rows(a_ln_b), c_conv_w)
    aw8 = jnp.broadcast_to(a_conv_w[:, :, None, :], (depth, CONV_WIDTH, SUBLANES, D_CONV))

    hs = x_sample.reshape(bd, D_MODEL)
    sk, sv, sa, sc = [], [], [], []
    for l in range(depth):
        q, kf, vf, zs, u, gza, bz, uc = _in_proj_sample(hs, lng, w_in_bf, cos_s, sin_s, qg, kg,
                                                        seg, l)
        attn = _sample_attn(q, kf, vf, cache_kt, cache_vt, l)
        hs, na, nc = _sample_out(hs, attn, zs, u, gza, bz, uc, state_conv_a, state_conv_c,
                                 *conv_w, w_out_bf, l)
        sk.append(kf)
        sv.append(vf)
        sa.append(na)
        sc.append(nc)
    new_cols = lambda new_rows: jnp.transpose(jnp.stack(new_rows), (0, 2, 1))

    assert depth >= 2
    shift_jobs = {depth - 2: (cache_kt, new_cols(sk)), depth - 1: (cache_vt, new_cols(sv))}
    shifted = {}

    hp = x_prompt
    pk, pv, pa, pc = [], [], [], []
    for l in range(depth):
        outs = _in_proj_prompt(hp, lng, w_in_bf, cos_p, sin_p, qg, kg, seg, l, PROMPT_TM_IN,
                               l_prompt, shift=shift_jobs.get(l))
        if l in shift_jobs:
            outs, shifted[l] = outs[:-1], outs[-1]
        n_pat = len(DILATIONS)
        q_d, k_d, v_d = outs[0:n_pat], outs[n_pat:2 * n_pat], outs[2 * n_pat:3 * n_pat]
        kt, vt, zs, u, gza, bz, uc = outs[3 * n_pat:]
        attn = _attention(list(zip(q_d, k_d, v_d)), bp, sp)
        hp = _out_proj(hp, attn, zs, u, gza, bz, uc, aw8, *conv_w[1:], w_out_bf, l,
                       PROMPT_TM_OUT)
        pk.append(kt)
        pv.append(vt)
        pa.append(u[:, sp - (CONV_WIDTH - 1):])
        pc.append(uc[:, sp - (SCONV_WIDTH - 1):])

    return (hp, hs.reshape(bd, ts, D_MODEL),
            _from_channel_major(jnp.stack(pk)), _from_channel_major(jnp.stack(pv)),
            jnp.stack(pa), jnp.stack(pc),
            _from_channel_major(shifted[depth - 2]), _from_channel_major(shifted[depth - 1]),
            jnp.stack(sa), jnp.stack(sc))
```
